```python
import math
import jax, jax.numpy as jnp
from jax import lax
import numpy as np

D_MODEL = 2048
BATCH = 1
SEQ = 8192
DEPTH = 4

CHUNK = 64
HGRN_HEADS = 8
HGRN_KDIM = 128
HGRN_VDIM = 128
HGRN_KWIDTH = HGRN_HEADS * HGRN_KDIM
HGRN_VWIDTH = HGRN_HEADS * HGRN_VDIM
DIFF_HEADS = 4
DIFF_HEAD_DIM = 128
DIFF_V_DIM = 2 * DIFF_HEAD_DIM
DIFF_QK_WIDTH = DIFF_HEADS * 2 * DIFF_HEAD_DIM
DIFF_V_WIDTH = DIFF_HEADS * DIFF_V_DIM
ROPE_THETA = 500000.0
ROPE_DIM = DIFF_HEAD_DIM // 4
Q_BLOCK = 128
FFN_HIDDEN = -(-8 * D_MODEL // (3 * 256)) * 256
IN_WIDTHS = (HGRN_KWIDTH, HGRN_KWIDTH, HGRN_VWIDTH, HGRN_VWIDTH,
             DIFF_QK_WIDTH, DIFF_QK_WIDTH, DIFF_V_WIDTH, D_MODEL, D_MODEL)
IN_TOTAL = sum(IN_WIDTHS)
NORM_EPS = 1e-6
SUBLN_EPS = 1e-5

kernel_name = "hgrn2_diffattn_gated_hybrid"


def rms_norm(x, w, eps=NORM_EPS):
    xf = x.astype(jnp.float32)
    y = xf * lax.rsqrt(jnp.mean(xf * xf, axis=-1, keepdims=True) + eps)
    return (y * w.astype(jnp.float32)).astype(x.dtype)


def split_indices():
    idx, acc = [], 0
    for w in IN_WIDTHS[:-1]:
        acc += w
        idx.append(acc)
    return idx


def rope_tables(seq):
    pos = jnp.arange(seq, dtype=jnp.float32)
    inv_freq = ROPE_THETA ** (-jnp.arange(0, ROPE_DIM, 2, dtype=jnp.float32) / ROPE_DIM)
    ang = pos[:, None] * inv_freq[None, :]
    return jnp.cos(ang), jnp.sin(ang)


def partial_rope(t, cos, sin):
    half = ROPE_DIM // 2
    c = cos[None, :, None, None, :]
    s = sin[None, :, None, None, :]
    t1 = t[..., :half].astype(jnp.float32)
    t2 = t[..., half:ROPE_DIM].astype(jnp.float32)
    out = jnp.concatenate([t1 * c - t2 * s, t2 * c + t1 * s,
                           t[..., ROPE_DIM:].astype(jnp.float32)], axis=-1)
    return out.astype(t.dtype)


def hgrn2_mixer(q_raw, f_raw, i_raw, g_raw, lb, gnorm_w):
    B, S, _ = q_raw.shape
    dt = q_raw.dtype
    H, K, V = HGRN_HEADS, HGRN_KDIM, HGRN_VDIM
    q = jax.nn.silu(q_raw.astype(jnp.float32)).reshape(B, S, H, K)
    fr = f_raw.astype(jnp.float32).reshape(B, S, H, K)
    log_lb = jnp.log(jnp.maximum(lb.astype(jnp.float32), jnp.finfo(jnp.float32).tiny)).reshape(H, K)
    log_f = jax.nn.log_sigmoid(fr) + jax.nn.softplus(log_lb - fr)
    k = -jnp.expm1(log_f)
    v = i_raw.astype(jnp.float32).reshape(B, S, H, V)
    nc = S // CHUNK

    def to_chunks(t):
        return t.reshape(B, nc, CHUNK, H, t.shape[-1]).transpose(1, 0, 3, 2, 4)

    causal = jnp.tril(jnp.ones((CHUNK, CHUNK), dtype=bool))

    def step(state, xs):
        qc, kc, vc, lfc = xs
        G = jnp.cumsum(lfc, axis=2)
        diff = G[:, :, :, None, :] - G[:, :, None, :, :]
        decay = jnp.exp(jnp.where(causal[:, :, None], diff, -jnp.inf))
        A = jnp.einsum('bhtk,bhsk,bhtsk->bhts', qc, kc, decay)
        o = (jnp.einsum('bhts,bhsv->bhtv', A, vc)
             + jnp.einsum('bhtk,bhkv->bhtv', qc * jnp.exp(G), state))
        G_last = G[:, :, -1:, :]
        state = (jnp.exp(G_last[:, :, 0, :])[..., None] * state
                 + jnp.einsum('bhsk,bhsv->bhkv', kc * jnp.exp(G_last - G), vc))
        return state, o

    s0 = jnp.zeros((B, H, K, V), jnp.float32)
    _, o = lax.scan(step, s0, (to_chunks(q), to_chunks(k), to_chunks(v), to_chunks(log_f)))
    o = o.transpose(1, 0, 3, 2, 4).reshape(B, S, H, V)
    g = g_raw.astype(jnp.float32).reshape(B, S, H, V)
    o = o * lax.rsqrt(jnp.mean(o * o, axis=-1, keepdims=True) + NORM_EPS)
    o = o * gnorm_w.astype(jnp.float32) * jax.nn.silu(g)
    return o.reshape(B, S, H * V).astype(dt)


def diff_attention(q_raw, k_raw, v_raw, lam, lam_init, subln_w, cos, sin):
    B, S, _ = q_raw.shape
    H, d = DIFF_HEADS, DIFF_HEAD_DIM
    q = partial_rope(q_raw.reshape(B, S, H, 2, d), cos, sin) * (d ** -0.5)
    k = partial_rope(k_raw.reshape(B, S, H, 2, d), cos, sin)
    v = v_raw.reshape(B, S, H, DIFF_V_DIM)
    nb = S // Q_BLOCK
    q_blocks = q.reshape(B, nb, Q_BLOCK, H, 2, d).transpose(1, 0, 2, 3, 4, 5)
    k_chunk = jnp.arange(S) // CHUNK
    q_chunk_blocks = k_chunk.reshape(nb, Q_BLOCK)

    def attend(args):
        qblk, qch = args
        s = jnp.einsum('bqhcd,bkhcd->bhcqk', qblk, k).astype(jnp.float32)
        mask = qch[:, None] >= k_chunk[None, :]
        p = jax.nn.softmax(jnp.where(mask, s, -jnp.inf), axis=-1)
        attn = (p[:, :, 0] - lam * p[:, :, 1]).astype(v.dtype)
        return jnp.einsum('bhqk,bkhe->bqhe', attn, v)

    o = lax.map(attend, (q_blocks, q_chunk_blocks))
    o = o.transpose(1, 0, 2, 3, 4).reshape(B, S, H, DIFF_V_DIM)
    o = rms_norm(o, subln_w, eps=SUBLN_EPS) * (1.0 - lam_init)
    return o.reshape(B, S, H * DIFF_V_DIM)


def setup_inputs(seed: int = 0) -> dict:
    key = jax.random.key(seed)
    ks = jax.random.split(key, 13)
    f32 = jnp.float32
    nrm = lambda k, shape, scale: jax.random.normal(k, shape, f32) * scale
    return {
        "x": nrm(ks[0], (BATCH, SEQ, D_MODEL), 1.0),
        "w_in": nrm(ks[1], (DEPTH, D_MODEL, IN_TOTAL), D_MODEL ** -0.5),
        "hgrn_lower_bounds": nrm(ks[2], (DEPTH, HGRN_KWIDTH), 0.5),
        "hgrn_gnorm_w": 1.0 + nrm(ks[3], (DEPTH, HGRN_VDIM), 0.02),
        "diff_lambda": nrm(ks[4], (DEPTH, 4, DIFF_HEAD_DIM), 0.1),
        "diff_subln_w": 1.0 + nrm(ks[5], (DEPTH, DIFF_V_DIM), 0.02),
        "w_branch_a": nrm(ks[6], (DEPTH, HGRN_VWIDTH, D_MODEL), HGRN_VWIDTH ** -0.5),
        "w_branch_b": nrm(ks[7], (DEPTH, DIFF_V_WIDTH, D_MODEL), DIFF_V_WIDTH ** -0.5),
        "w_out": nrm(ks[8], (DEPTH, D_MODEL, D_MODEL), D_MODEL ** -0.5),
        "norm_w": 1.0 + nrm(ks[9], (DEPTH, 4, D_MODEL), 0.02),
        "w_ffn_in": nrm(ks[10], (DEPTH, D_MODEL, 2 * FFN_HIDDEN), D_MODEL ** -0.5),
        "w_ffn_out": nrm(ks[11], (DEPTH, FFN_HIDDEN, D_MODEL), FFN_HIDDEN ** -0.5),
    }


def reference(x, w_in, hgrn_lower_bounds, hgrn_gnorm_w, diff_lambda, diff_subln_w,
              w_branch_a, w_branch_b, w_out, norm_w, w_ffn_in, w_ffn_out):
    S = x.shape[1]
    cos, sin = rope_tables(S)
    lb_p = jax.nn.softmax(hgrn_lower_bounds.astype(jnp.float32), axis=0)
    lb_all = jnp.cumsum(lb_p, axis=0) - lb_p[0:1]
    splits = split_indices()
    for l in range(DEPTH):
        nw = norm_w[l]
        h = rms_norm(x, nw[0])
        proj = h @ w_in[l]
        hq, hf, hi, hg, dq, dk, dv, ga, gb = jnp.split(proj, splits, axis=-1)
        y_a = hgrn2_mixer(hq, hf, hi, hg, lb_all[l], hgrn_gnorm_w[l]) @ w_branch_a[l]
        lam_init = 0.8 - 0.6 * math.exp(-0.3 * l)
        lp = diff_lambda[l].astype(jnp.float32)
        lam = jnp.exp(jnp.sum(lp[0] * lp[1])) - jnp.exp(jnp.sum(lp[2] * lp[3])) + lam_init
        y_b = diff_attention(dq, dk, dv, lam, lam_init, diff_subln_w[l], cos, sin) @ w_branch_b[l]
        merged = jax.nn.sigmoid(ga) * y_a + jax.nn.sigmoid(gb) * y_b
        x = x + rms_norm(merged @ w_out[l], nw[1])
        h = rms_norm(x, nw[2])
        gate, up = jnp.split(h @ w_ffn_in[l], 2, axis=-1)
        x = x + rms_norm((jax.nn.silu(gate) * up) @ w_ffn_out[l], nw[3])
    return x
```

```python
import functools
import math

import jax
import jax.numpy as jnp
from jax import lax
from jax.experimental import pallas as pl
from jax.experimental.pallas import tpu as pltpu

F32 = jnp.float32
BF16 = jnp.bfloat16

CHUNK = 64
HGRN_HEADS = 8
HGRN_DIM = 128
HGRN_WIDTH = HGRN_HEADS * HGRN_DIM
DIFF_HEADS = 4
DIFF_HEAD_DIM = 128
DIFF_V_DIM = 2 * DIFF_HEAD_DIM
ROPE_THETA = 500000.0
ROPE_DIM = DIFF_HEAD_DIM // 4
NORM_EPS = 1e-6
SUBLN_EPS = 1e-5
LANES = 128

HGRN_CHUNK = 64
HGRN_FAST_MAX_DECAY = 80.0
MASK_VALUE = -1e30

VMEM_LIMIT = 56 * 1024 * 1024


def _rms(x, eps):
    return x * lax.rsqrt(jnp.mean(x * x, axis=-1, keepdims=True) + eps)


def _softplus(y):
    return jnp.maximum(y, 0.0) + jnp.log1p(jnp.exp(-jnp.abs(y)))


def _inproj_kernel(x_ref, nw_ref, w_ref, tab_ref, o_ref, h_ref, *, rope_lo, rope_hi):
    j = pl.program_id(1)

    @pl.when(j == 0)
    def _():
        x = x_ref[...]
        h_ref[...] = (_rms(x, NORM_EPS) * nw_ref[...]).astype(BF16)

    acc = jnp.dot(h_ref[...], w_ref[...], preferred_element_type=F32)
    is_rope = jnp.logical_and(j >= rope_lo, j < rope_hi)

    @pl.when(is_rope)
    def _():
        c = tab_ref[0, 0]
        s_dn = tab_ref[0, 1]
        s_up = tab_ref[0, 2]
        for g in range(acc.shape[1] // LANES):
            t = acc[:, g * LANES:(g + 1) * LANES]
            r = (t * c + pltpu.roll(t, LANES - ROPE_DIM // 2, 1) * s_dn
                 + pltpu.roll(t, ROPE_DIM // 2, 1) * s_up)
            o_ref[:, g * LANES:(g + 1) * LANES] = r.astype(o_ref.dtype)

    @pl.when(jnp.logical_not(is_rope))
    def _():
        o_ref[...] = acc.astype(o_ref.dtype)


def _in_proj(x, nw, w, tab, *, tm, tn, q_col, k_col, v_col):
    s, d = x.shape
    n = w.shape[1]
    assert q_col % tn == 0 and k_col - q_col == tn and v_col - k_col == tn
    q_blk = q_col // tn
    return pl.pallas_call(
        functools.partial(_inproj_kernel, rope_lo=q_blk, rope_hi=q_blk + 2),
        grid=(s // tm, n // tn),
        in_specs=[
            pl.BlockSpec((tm, d), lambda i, j: (i, 0)),
            pl.BlockSpec((1, d), lambda i, j: (0, 0)),
            pl.BlockSpec((d, tn), lambda i, j: (0, j)),
            pl.BlockSpec((1, 3, tm, LANES),
                         lambda i, j: (jnp.clip(j - q_blk, 0, 1), 0, i, 0)),
        ],
        out_specs=pl.BlockSpec((tm, tn), lambda i, j: (i, j)),
        out_shape=jax.ShapeDtypeStruct((s, n), BF16),
        scratch_shapes=[pltpu.VMEM((tm, d), BF16)],
        compiler_params=pltpu.CompilerParams(
            dimension_semantics=("parallel", "arbitrary"), vmem_limit_bytes=VMEM_LIMIT),
        name="in_proj",
    )(x, nw, w, tab)


def _hgrn_kernel(lbraw_ref, gw_ref, q_ref, f_ref, i_ref, g_ref, o_ref,
                 state_ref, loglb_ref, osc_ref, gsc_ref, ksc_ref, vsc_ref, *, layer, tt):
    c_len = HGRN_CHUNK
    n_heads = HGRN_HEADS
    dh = HGRN_DIM

    @pl.when(pl.program_id(0) == 0)
    def _():
        state_ref[...] = jnp.zeros_like(state_ref)
        gsc_ref[...] = jnp.zeros_like(gsc_ref)
        ksc_ref[...] = jnp.zeros_like(ksc_ref)
        vsc_ref[...] = jnp.zeros_like(vsc_ref)
        raw = lbraw_ref[...]
        depth = raw.shape[0]
        mx = raw[0:1]
        for r in range(1, depth):
            mx = jnp.maximum(mx, raw[r:r + 1])
        es = [jnp.exp(raw[r:r + 1] - mx) for r in range(depth)]
        tot = es[0]
        for r in range(1, depth):
            tot = tot + es[r]
        lb = jnp.zeros_like(tot)
        for r in range(1, layer + 1):
            lb = lb + es[r] / tot
        loglb_ref[...] = jnp.log(jnp.maximum(lb, jnp.finfo(F32).tiny))

    row = lax.broadcasted_iota(jnp.int32, (c_len, c_len), 0)
    col = lax.broadcasted_iota(jnp.int32, (c_len, c_len), 1)
    causal = row >= col
    tril = causal.astype(F32)
    log_lb = loglb_ref[...]
    gw = gw_ref[...]

    def chunk_body(c, carry):
        r0 = pl.multiple_of(c * c_len, c_len)
        qr = q_ref[pl.ds(r0, c_len), :].astype(F32)
        fr = f_ref[pl.ds(r0, c_len), :].astype(F32)
        v = i_ref[pl.ds(r0, c_len), :].astype(F32)
        gr = g_ref[pl.ds(r0, c_len), :].astype(F32)
        q = qr * jax.nn.sigmoid(qr)
        lf = -_softplus(-fr) + _softplus(log_lb - fr)
        kk = 1.0 - jnp.exp(lf)
        gcum = jnp.dot(tril, lf, precision=lax.Precision.HIGHEST, preferred_element_type=F32)
        glast = gcum[c_len - 1:c_len, :]
        qg = q * jnp.exp(gcum)
        kdec = (kk * jnp.exp(glast - gcum)).astype(BF16)
        v16 = v.astype(BF16)
        qg16 = qg.astype(BF16)
        fast = jnp.min(glast) > -HGRN_FAST_MAX_DECAY

        @pl.when(fast)
        def _():
            kinv = (kk * jnp.exp(-gcum)).astype(BF16)
            for h in range(n_heads):
                sl = slice(h * dh, (h + 1) * dh)
                a = lax.dot_general(qg16[:, sl], kinv[:, sl], (((1,), (1,)), ((), ())),
                                    preferred_element_type=F32)
                a = jnp.where(causal, a, 0.0).astype(BF16)
                osc_ref[:, sl] = jnp.dot(a, v16[:, sl], preferred_element_type=F32)

        @pl.when(jnp.logical_not(fast))
        def _():
            gsc_ref[c_len:, :] = gcum
            ksc_ref[c_len:, :] = kk
            vsc_ref[c_len:, :] = v
            osc_ref[...] = jnp.zeros_like(osc_ref)

            def off_body(d, carry2):
                gs = pltpu.roll(gsc_ref[...], d, 0)[c_len:, :]
                ks = pltpu.roll(ksc_ref[...], d, 0)[c_len:, :]
                vs = pltpu.roll(vsc_ref[...], d, 0)[c_len:, :]
                p = q * ks * jnp.exp(jnp.minimum(gcum - gs, 0.0))
                for h in range(n_heads):
                    sl = slice(h * dh, (h + 1) * dh)
                    a = jnp.sum(p[:, sl], axis=1, keepdims=True)
                    osc_ref[:, sl] += a * vs[:, sl]
                return carry2

            lax.fori_loop(0, c_len, off_body, 0)

        eg_last = jnp.exp(glast)
        for h in range(n_heads):
            sl = slice(h * dh, (h + 1) * dh)
            st = state_ref[h]
            o = osc_ref[:, sl] + lax.dot_general(
                qg16[:, sl], st.astype(BF16), (((1,), (1,)), ((), ())),
                preferred_element_type=F32)
            state_ref[h] = st * eg_last[:, sl] + lax.dot_general(
                v16[:, sl], kdec[:, sl], (((0,), (0,)), ((), ())), preferred_element_type=F32)
            gh = gr[:, sl]
            o = _rms(o, NORM_EPS) * gw * (gh * jax.nn.sigmoid(gh))
            o_ref[pl.ds(r0, c_len), sl] = o.astype(o_ref.dtype)
        return carry

    lax.fori_loop(0, tt // c_len, chunk_body, 0)


def _hgrn(proj, lb_raw, gnorm_w, *, layer, tt):
    s = proj.shape[0]
    w = HGRN_WIDTH
    depth = lb_raw.shape[0]
    col_spec = lambda b: pl.BlockSpec((tt, w), lambda i, b=b: (i, b))
    return pl.pallas_call(
        functools.partial(_hgrn_kernel, layer=layer, tt=tt),
        grid=(s // tt,),
        in_specs=[
            pl.BlockSpec((depth, w), lambda i: (0, 0)),
            pl.BlockSpec((1, HGRN_DIM), lambda i: (0, 0)),
            col_spec(0), col_spec(1), col_spec(2), col_spec(3),
        ],
        out_specs=pl.BlockSpec((tt, w), lambda i: (i, 0)),
        out_shape=jax.ShapeDtypeStruct((s, w), BF16),
        scratch_shapes=[
            pltpu.VMEM((HGRN_HEADS, HGRN_DIM, HGRN_DIM), F32),
            pltpu.VMEM((1, w), F32),
            pltpu.VMEM((HGRN_CHUNK, w), F32),
            pltpu.VMEM((2 * HGRN_CHUNK, w), F32),
            pltpu.VMEM((2 * HGRN_CHUNK, w), F32),
            pltpu.VMEM((2 * HGRN_CHUNK, w), F32),
        ],
        compiler_params=pltpu.CompilerParams(
            dimension_semantics=("arbitrary",), vmem_limit_bytes=VMEM_LIMIT),
        name="hgrn",
    )(lb_raw, gnorm_w, proj, proj, proj, proj)


def _attn_kernel(lam_ref, sw_ref, q_ref, k_ref, v_ref, o_ref, *, lam_init, tq):
    qi = pl.program_id(1)
    d = DIFF_HEAD_DIM
    lp = lam_ref[...]
    lam = (jnp.exp(jnp.sum(lp[0:1] * lp[1:2], axis=1, keepdims=True))
           - jnp.exp(jnp.sum(lp[2:3] * lp[3:4], axis=1, keepdims=True)) + lam_init)
    q = q_ref[...]
    q0 = q[:, :d]
    q1 = q[:, d:]
    row = lax.broadcasted_iota(jnp.int32, (tq, tq), 0) // CHUNK
    col = lax.broadcasted_iota(jnp.int32, (tq, tq), 1) // CHUNK
    visible = row >= col

    def online(s, v, m, l, acc):
        m_new = jnp.maximum(m, jnp.max(s, axis=1, keepdims=True))
        alpha = jnp.exp(m - m_new)
        p = jnp.exp(s - m_new)
        l_new = alpha * l + jnp.sum(p, axis=1, keepdims=True)
        acc_new = alpha * acc + jnp.dot(p.astype(BF16), v, preferred_element_type=F32)
        return m_new, l_new, acc_new

    def step(kb, carry, masked):
        m0, l0, a0, m1, l1, a1 = carry
        r0 = pl.multiple_of(kb * tq, tq)
        k = k_ref[pl.ds(r0, tq), :]
        v = v_ref[pl.ds(r0, tq), :]
        nt = (((1,), (1,)), ((), ()))
        s0 = lax.dot_general(q0, k[:, :d], nt, preferred_element_type=F32)
        s1 = lax.dot_general(q1, k[:, d:], nt, preferred_element_type=F32)
        if masked:
            s0 = jnp.where(visible, s0, MASK_VALUE)
            s1 = jnp.where(visible, s1, MASK_VALUE)
        m0, l0, a0 = online(s0, v, m0, l0, a0)
        m1, l1, a1 = online(s1, v, m1, l1, a1)
        return m0, l0, a0, m1, l1, a1

    minit = jnp.full((tq, 1), MASK_VALUE, F32)
    linit = jnp.zeros((tq, 1), F32)
    ainit = jnp.zeros((tq, DIFF_V_DIM), F32)
    carry = (minit, linit, ainit, minit, linit, ainit)
    carry = lax.fori_loop(0, qi, functools.partial(step, masked=False), carry)
    m0, l0, a0, m1, l1, a1 = step(qi, carry, True)
    o = a0 / l0 - lam * (a1 / l1)
    o = _rms(o, SUBLN_EPS) * sw_ref[...] * (1.0 - lam_init)
    o_ref[...] = o.astype(o_ref.dtype)


def _attn(proj, lam_params, subln_w, *, lam_init, tq, q_col, k_col, v_col):
    s = proj.shape[0]
    hw = 2 * DIFF_HEAD_DIM
    qb, kb, vb = q_col // hw, k_col // hw, v_col // hw
    return pl.pallas_call(
        functools.partial(_attn_kernel, lam_init=lam_init, tq=tq),
        grid=(DIFF_HEADS, s // tq),
        in_specs=[
            pl.BlockSpec(lam_params.shape, lambda h, i: (0, 0)),
            pl.BlockSpec((1, DIFF_V_DIM), lambda h, i: (0, 0)),
            pl.BlockSpec((tq, hw), lambda h, i: (i, qb + h)),
            pl.BlockSpec((s, hw), lambda h, i: (0, kb + h)),
            pl.BlockSpec((s, DIFF_V_DIM), lambda h, i: (0, vb + h)),
        ],
        out_specs=pl.BlockSpec((tq, DIFF_V_DIM), lambda h, i: (i, h)),
        out_shape=jax.ShapeDtypeStruct((s, DIFF_HEADS * DIFF_V_DIM), BF16),
        compiler_params=pltpu.CompilerParams(
            dimension_semantics=("parallel", "arbitrary"), vmem_limit_bytes=VMEM_LIMIT),
        name="diff_attn",
    )(lam_params, subln_w, proj, proj, proj)


def _merge_kernel(x_ref, oa_ref, ob_ref, ga0_ref, ga1_ref, gb0_ref, gb1_ref,
                  wa_ref, wb_ref, wo_ref, nw_ref, out_ref, mg_ref):
    half = ga0_ref.shape[1]
    oa = oa_ref[...]
    ob = ob_ref[...]
    for part, (ga_ref, gb_ref) in enumerate(((ga0_ref, gb0_ref), (ga1_ref, gb1_ref))):
        sl = slice(part * half, (part + 1) * half)
        ya = jnp.dot(oa, wa_ref[:, sl], preferred_element_type=F32)
        yb = jnp.dot(ob, wb_ref[:, sl], preferred_element_type=F32)
        merged = (jax.nn.sigmoid(ga_ref[...].astype(F32)) * ya
                  + jax.nn.sigmoid(gb_ref[...].astype(F32)) * yb)
        mg_ref[:, sl] = merged.astype(BF16)
    z = jnp.dot(mg_ref[...], wo_ref[...], preferred_element_type=F32)
    out_ref[...] = x_ref[...] + _rms(z, NORM_EPS) * nw_ref[...]


def _merge(x, o_a, o_b, proj, w_a, w_b, w_o, nw, *, tm, gate_col):
    s, d = x.shape
    half = d // 2
    gb = gate_col // half
    wa_k, wb_k = w_a.shape[0], w_b.shape[0]
    const = lambda shape: pl.BlockSpec(shape, lambda i: (0, 0), pipeline_mode=pl.Buffered(1))
    gate = lambda b: pl.BlockSpec((tm, half), lambda i, b=b: (i, b))
    return pl.pallas_call(
        _merge_kernel,
        grid=(s // tm,),
        in_specs=[
            pl.BlockSpec((tm, d), lambda i: (i, 0)),
            pl.BlockSpec((tm, wa_k), lambda i: (i, 0)),
            pl.BlockSpec((tm, wb_k), lambda i: (i, 0)),
            gate(gb), gate(gb + 1), gate(gb + 2), gate(gb + 3),
            const((wa_k, d)), const((wb_k, d)), const((d, d)), const((1, d)),
        ],
        out_specs=pl.BlockSpec((tm, d), lambda i: (i, 0)),
        out_shape=jax.ShapeDtypeStruct((s, d), F32),
        scratch_shapes=[pltpu.VMEM((tm, d), BF16)],
        compiler_params=pltpu.CompilerParams(
            dimension_semantics=("parallel",), vmem_limit_bytes=VMEM_LIMIT),
        name="merge",
    )(x, o_a, o_b, proj, proj, proj, proj, w_a, w_b, w_o, nw)


def _ffn_kernel(x_ref, nwi_ref, nwo_ref, wg_ref, wu_ref, wd_ref, out_ref, h_ref, acc_ref):
    j = pl.program_id(1)

    @pl.when(j == 0)
    def _():
        h_ref[...] = (_rms(x_ref[...], NORM_EPS) * nwi_ref[...]).astype(BF16)
        acc_ref[...] = jnp.zeros_like(acc_ref)

    h = h_ref[...]
    g = jnp.dot(h, wg_ref[...], preferred_element_type=F32)
    u = jnp.dot(h, wu_ref[...], preferred_element_type=F32)
    a = (g * jax.nn.sigmoid(g) * u).astype(BF16)
    acc_ref[...] += jnp.dot(a, wd_ref[...], preferred_element_type=F32)

    @pl.when(j == pl.num_programs(1) - 1)
    def _():
        out_ref[...] = x_ref[...] + _rms(acc_ref[...], NORM_EPS) * nwo_ref[...]


def _ffn(x, nwi, nwo, w_in, w_out, *, tm, th):
    s, d = x.shape
    hidden = w_out.shape[0]
    nh = hidden // th
    return pl.pallas_call(
        _ffn_kernel,
        grid=(s // tm, nh),
        in_specs=[
            pl.BlockSpec((tm, d), lambda i, j: (i, 0)),
            pl.BlockSpec((1, d), lambda i, j: (0, 0)),
            pl.BlockSpec((1, d), lambda i, j: (0, 0)),
            pl.BlockSpec((d, th), lambda i, j: (0, j)),
            pl.BlockSpec((d, th), lambda i, j: (0, nh + j)),
            pl.BlockSpec((th, d), lambda i, j: (j, 0)),
        ],
        out_specs=pl.BlockSpec((tm, d), lambda i, j: (i, 0)),
        out_shape=jax.ShapeDtypeStruct((s, d), F32),
        scratch_shapes=[pltpu.VMEM((tm, d), BF16), pltpu.VMEM((tm, d), F32)],
        compiler_params=pltpu.CompilerParams(
            dimension_semantics=("parallel", "arbitrary"), vmem_limit_bytes=VMEM_LIMIT),
        name="ffn",
    )(x, nwi, nwo, w_in, w_in, w_out)


def _rope_tables(seq):
    half = ROPE_DIM // 2
    pos = jnp.arange(seq, dtype=F32)
    inv_freq = ROPE_THETA ** (-jnp.arange(0, ROPE_DIM, 2, dtype=F32) / ROPE_DIM)
    ang = pos[:, None] * inv_freq[None, :]
    cos, sin = jnp.cos(ang), jnp.sin(ang)
    zeros = jnp.zeros((seq, LANES - ROPE_DIM), F32)
    zh = jnp.zeros((seq, half), F32)
    c = jnp.concatenate([cos, cos, jnp.ones_like(zeros)], axis=1)
    s_dn = jnp.concatenate([-sin, zh, zeros], axis=1)
    s_up = jnp.concatenate([zh, sin, zeros], axis=1)
    k_tab = jnp.stack([c, s_dn, s_up])
    return jnp.stack([k_tab * (DIFF_HEAD_DIM ** -0.5), k_tab])


def kernel(x, w_in, hgrn_lower_bounds, hgrn_gnorm_w, diff_lambda, diff_subln_w,
           w_branch_a, w_branch_b, w_out, norm_w, w_ffn_in, w_ffn_out):
    batch, seq, d_model = x.shape
    assert batch == 1
    depth = w_in.shape[0]
    q_col = 4 * HGRN_WIDTH
    k_col = q_col + DIFF_HEADS * 2 * DIFF_HEAD_DIM
    v_col = k_col + DIFF_HEADS * 2 * DIFF_HEAD_DIM
    gate_col = v_col + DIFF_HEADS * DIFF_V_DIM
    tab = _rope_tables(seq)
    xs = x[0]
    lb_raw = hgrn_lower_bounds.astype(F32)
    for l in range(depth):
        nw = norm_w[l].astype(F32)
        lam_init = 0.8 - 0.6 * math.exp(-0.3 * l)
        proj = _in_proj(xs, nw[0:1], w_in[l].astype(BF16), tab,
                        tm=min(1024, seq), tn=1024, q_col=q_col, k_col=k_col, v_col=v_col)
        o_a = _hgrn(proj, lb_raw, hgrn_gnorm_w[l].astype(F32)[None, :],
                    layer=l, tt=min(256, seq))
        o_b = _attn(proj, diff_lambda[l].astype(F32), diff_subln_w[l].astype(F32)[None, :],
                    lam_init=lam_init, tq=min(512, seq), q_col=q_col, k_col=k_col, v_col=v_col)
        xs = _merge(xs, o_a, o_b, proj, w_branch_a[l].astype(BF16), w_branch_b[l].astype(BF16),
                    w_out[l].astype(BF16), nw[1:2], tm=min(256, seq), gate_col=gate_col)
        xs = _ffn(xs, nw[2:3], nw[3:4], w_ffn_in[l].astype(BF16), w_ffn_out[l].astype(BF16),
                  tm=min(512, seq), th=512)
    return xs[None]
```

```python
import functools
import math

import jax
import jax.numpy as jnp
from jax import lax
from jax.experimental import pallas as pl
from jax.experimental.pallas import tpu as pltpu

F32 = jnp.float32
BF16 = jnp.bfloat16

CHUNK = 64
HGRN_HEADS = 8
HGRN_DIM = 128
HGRN_WIDTH = HGRN_HEADS * HGRN_DIM
DIFF_HEADS = 4
DIFF_HEAD_DIM = 128
DIFF_V_DIM = 2 * DIFF_HEAD_DIM
ROPE_THETA = 500000.0
ROPE_DIM = DIFF_HEAD_DIM // 4
NORM_EPS = 1e-6
SUBLN_EPS = 1e-5
LANES = 128
SUBLANES = 8

HGRN_CHUNK = 64
HGRN_FAST_MAX_DECAY = 80.0
MASK_VALUE = -1e30
NORMS_PER_LAYER = 4

VMEM_LIMIT = 56 * 1024 * 1024

NT_DIMS = (((1,), (1,)), ((), ()))
TN_DIMS = (((0,), (0,)), ((), ()))


def _rms(x, eps):
    return x * lax.rsqrt(jnp.mean(x * x, axis=-1, keepdims=True) + eps)


def _silu(x):
    return x * jax.nn.sigmoid(x)


def _params(*sem):
    return pltpu.CompilerParams(dimension_semantics=sem, vmem_limit_bytes=VMEM_LIMIT)


def _norm_kernel(x_ref, nw_ref, h_ref):
    h_ref[...] = (_rms(x_ref[...], NORM_EPS) * nw_ref[...]).astype(h_ref.dtype)


def _norm(x, nw, *, tm):
    s, d = x.shape
    return pl.pallas_call(
        _norm_kernel,
        grid=(s // tm,),
        in_specs=[pl.BlockSpec((tm, d), lambda i: (i, 0)), pl.BlockSpec((1, d), lambda i: (0, 0))],
        out_specs=pl.BlockSpec((tm, d), lambda i: (i, 0)),
        out_shape=jax.ShapeDtypeStruct((s, d), BF16),
        compiler_params=_params("parallel"),
        name="pre_norm",
    )(x, nw)


def _inproj_kernel(h_ref, w_ref, tab_ref, o_ref, wb_ref, *, rope_lo, rope_hi):
    j = pl.program_id(0)

    @pl.when(pl.program_id(1) == 0)
    def _():
        wb_ref[...] = w_ref[...].astype(BF16)

    acc = jnp.dot(h_ref[...], wb_ref[...], preferred_element_type=F32)
    is_rope = jnp.logical_and(j >= rope_lo, j < rope_hi)

    @pl.when(is_rope)
    def _():
        c = tab_ref[0, 0]
        s_dn = tab_ref[0, 1]
        s_up = tab_ref[0, 2]
        for g in range(acc.shape[1] // LANES):
            t = acc[:, g * LANES:(g + 1) * LANES]
            r = (t * c + pltpu.roll(t, LANES - ROPE_DIM // 2, 1) * s_dn
                 + pltpu.roll(t, ROPE_DIM // 2, 1) * s_up)
            o_ref[:, g * LANES:(g + 1) * LANES] = r.astype(o_ref.dtype)

    @pl.when(jnp.logical_not(is_rope))
    def _():
        o_ref[...] = acc.astype(o_ref.dtype)


def _in_proj(h, w_all, tab, *, layer, tm, tn, q_col, k_col, v_col):
    s, d = h.shape
    n = w_all.shape[2]
    assert q_col % tn == 0 and k_col - q_col == tn and v_col - k_col == tn
    q_blk = q_col // tn

    def tab_map(j, i):
        is_rope = jnp.logical_and(j >= q_blk, j < q_blk + 2)
        return (jnp.clip(j - q_blk, 0, 1), 0, jnp.where(is_rope, i, 0), 0)

    return pl.pallas_call(
        functools.partial(_inproj_kernel, rope_lo=q_blk, rope_hi=q_blk + 2),
        grid=(n // tn, s // tm),
        in_specs=[
            pl.BlockSpec((tm, d), lambda j, i: (i, 0)),
            pl.BlockSpec((None, d, tn), lambda j, i: (layer, 0, j)),
            pl.BlockSpec((1, 3, tm, LANES), tab_map),
        ],
        out_specs=pl.BlockSpec((tm, tn), lambda j, i: (i, j)),
        out_shape=jax.ShapeDtypeStruct((s, n), BF16),
        scratch_shapes=[pltpu.VMEM((d, tn), BF16)],
        compiler_params=_params("arbitrary", "arbitrary"),
        name="in_proj",
    )(h, w_all, tab)


def _chunk_cumsum(x):
    groups = x.shape[0] // SUBLANES
    x3 = x.reshape(groups, SUBLANES, x.shape[1])
    sub = lax.broadcasted_iota(jnp.int32, (1, SUBLANES, x.shape[1]), 1)
    for shift in (1, 2, 4):
        x3 = x3 + jnp.where(sub >= shift, pltpu.roll(x3, shift, 1), 0.0)
    tot = x3[:, SUBLANES - 1:SUBLANES, :]
    offs = [jnp.zeros_like(tot[0:1])]
    for g in range(1, groups):
        offs.append(offs[-1] + tot[g - 1:g])
    return (x3 + jnp.concatenate(offs, axis=0)).reshape(x.shape)


def _hgrn_kernel(lbraw_ref, gw_ref, q_ref, f_ref, i_ref, g_ref, o_ref,
                 state_ref, lbp_ref, qg_ref, kinv_ref, kdec_ref, eg_ref,
                 osc_ref, gsc_ref, ksc_ref, vsc_ref, *, layer, tt):
    c_len = HGRN_CHUNK
    n_chunks = tt // c_len
    dh = HGRN_DIM
    width = HGRN_WIDTH

    @pl.when(pl.program_id(0) == 0)
    def _():
        state_ref[...] = jnp.zeros_like(state_ref)
        gsc_ref[...] = jnp.zeros_like(gsc_ref)
        ksc_ref[...] = jnp.zeros_like(ksc_ref)
        vsc_ref[...] = jnp.zeros_like(vsc_ref)
        raw = lbraw_ref[...]
        depth = raw.shape[0]
        mx = raw[0:1]
        for r in range(1, depth):
            mx = jnp.maximum(mx, raw[r:r + 1])
        es = [jnp.exp(raw[r:r + 1] - mx) for r in range(depth)]
        tot = es[0]
        for r in range(1, depth):
            tot = tot + es[r]
        lb = jnp.zeros_like(tot)
        for r in range(1, layer + 1):
            lb = lb + es[r] / tot
        lbp_ref[...] = jnp.maximum(lb, jnp.finfo(F32).tiny)

    causal = (lax.broadcasted_iota(jnp.int32, (c_len, c_len), 0)
              >= lax.broadcasted_iota(jnp.int32, (c_len, c_len), 1))
    lbp = lbp_ref[...]
    gw = gw_ref[...]

    def chunk_rows(c):
        return pl.ds(pl.multiple_of(c * c_len, c_len), c_len)

    def gates(rows):
        qr = q_ref[rows, :].astype(F32)
        fr = f_ref[rows, :].astype(F32)
        t = jnp.exp(-jnp.abs(fr))
        pos = fr >= 0.0
        inv = 1.0 / (1.0 + t)
        f = jnp.where(pos, 1.0 + lbp * t, t + lbp) * inv
        kk = jnp.where(pos, t, 1.0) * inv * (1.0 - lbp)
        return _silu(qr), kk, _chunk_cumsum(jnp.log2(f))

    def prepare(c, gmin):
        rows = chunk_rows(c)
        q, kk, gcum = gates(rows)
        glast = gcum[c_len - 1:c_len, :]
        qg_ref[rows, :] = (q * jnp.exp2(gcum)).astype(BF16)
        kinv_ref[rows, :] = (kk * jnp.exp2(-gcum)).astype(BF16)
        kdec_ref[rows, :] = (kk * jnp.exp2(glast - gcum)).astype(BF16)
        eg_ref[c] = jnp.exp2(glast)
        return jnp.minimum(gmin, glast)

    gmin = lax.fori_loop(0, n_chunks, prepare, jnp.zeros((1, width), F32))
    fast = jnp.min(gmin) > -HGRN_FAST_MAX_DECAY * math.log2(math.e)

    def finish_head(c, rows, h, o_intra):
        sl = slice(h * dh, (h + 1) * dh)
        st = state_ref[h]
        qg = qg_ref[rows, sl]
        v = i_ref[rows, sl]
        o = o_intra + lax.dot_general(qg, st.astype(BF16), NT_DIMS, preferred_element_type=F32)
        state_ref[h] = st * eg_ref[c, :, sl] + lax.dot_general(
            v, kdec_ref[rows, sl], TN_DIMS, preferred_element_type=F32)
        gate = _silu(g_ref[rows, sl].astype(F32))
        o_ref[rows, sl] = (_rms(o, NORM_EPS) * gw * gate).astype(o_ref.dtype)

    def fast_chunk(c, carry):
        rows = chunk_rows(c)
        for h in range(HGRN_HEADS):
            sl = slice(h * dh, (h + 1) * dh)
            a = lax.dot_general(qg_ref[rows, sl], kinv_ref[rows, sl], NT_DIMS,
                                preferred_element_type=F32)
            a = jnp.where(causal, a, 0.0).astype(BF16)
            finish_head(c, rows, h, jnp.dot(a, i_ref[rows, sl], preferred_element_type=F32))
        return carry

    def slow_chunk(c, carry):
        rows = chunk_rows(c)
        q, kk, gcum = gates(rows)
        gsc_ref[c_len:, :] = gcum
        ksc_ref[c_len:, :] = kk
        vsc_ref[c_len:, :] = i_ref[rows, :].astype(F32)
        osc_ref[...] = jnp.zeros_like(osc_ref)

        def off_body(d, carry2):
            gs = pltpu.roll(gsc_ref[...], d, 0)[c_len:, :]
            ks = pltpu.roll(ksc_ref[...], d, 0)[c_len:, :]
            vs = pltpu.roll(vsc_ref[...], d, 0)[c_len:, :]
            p = q * ks * jnp.exp2(jnp.minimum(gcum - gs, 0.0))
            for h in range(HGRN_HEADS):
                sl = slice(h * dh, (h + 1) * dh)
                osc_ref[:, sl] += jnp.sum(p[:, sl], axis=1, keepdims=True) * vs[:, sl]
            return carry2

        lax.fori_loop(0, c_len, off_body, 0)
        for h in range(HGRN_HEADS):
            finish_head(c, rows, h, osc_ref[:, h * dh:(h + 1) * dh])
        return carry

    @pl.when(fast)
    def _():
        lax.fori_loop(0, n_chunks, fast_chunk, 0, unroll=2)

    @pl.when(jnp.logical_not(fast))
    def _():
        lax.fori_loop(0, n_chunks, slow_chunk, 0)


def _hgrn(proj, lb_raw, gnorm_w, *, layer, tt):
    s = proj.shape[0]
    w = HGRN_WIDTH
    depth = lb_raw.shape[0]
    col_spec = lambda b: pl.BlockSpec((tt, w), lambda i, b=b: (i, b))
    return pl.pallas_call(
        functools.partial(_hgrn_kernel, layer=layer, tt=tt),
        grid=(s // tt,),
        in_specs=[
            pl.BlockSpec((depth, w), lambda i: (0, 0)),
            pl.BlockSpec((1, HGRN_DIM), lambda i: (0, 0)),
            col_spec(0), col_spec(1), col_spec(2), col_spec(3),
        ],
        out_specs=pl.BlockSpec((tt, w), lambda i: (i, 0)),
        out_shape=jax.ShapeDtypeStruct((s, w), BF16),
        scratch_shapes=[
            pltpu.VMEM((HGRN_HEADS, HGRN_DIM, HGRN_DIM), F32),
            pltpu.VMEM((1, w), F32),
            pltpu.VMEM((tt, w), BF16),
            pltpu.VMEM((tt, w), BF16),
            pltpu.VMEM((tt, w), BF16),
            pltpu.VMEM((tt // HGRN_CHUNK, 1, w), F32),
            pltpu.VMEM((HGRN_CHUNK, w), F32),
            pltpu.VMEM((2 * HGRN_CHUNK, w), F32),
            pltpu.VMEM((2 * HGRN_CHUNK, w), F32),
            pltpu.VMEM((2 * HGRN_CHUNK, w), F32),
        ],
        compiler_params=_params("arbitrary"),
        name="hgrn",
    )(lb_raw, gnorm_w, proj, proj, proj, proj)


def _attn_kernel(lam_ref, sw_ref, q_ref, k_ref, v_ref, o_ref,
                 s_ref, p_ref, m_ref, l_ref, al_ref, acc_ref, *, lam_init, tq, rb):
    qi = pl.program_id(1)
    d = DIFF_HEAD_DIM
    n_col = tq // LANES
    m_ref[...] = jnp.full_like(m_ref, MASK_VALUE)
    l_ref[...] = jnp.zeros_like(l_ref)
    acc_ref[...] = jnp.zeros_like(acc_ref)
    q = q_ref[...]

    def softmax_rows(masked):
        def body(r, carry):
            r0 = pl.multiple_of(r * rb, rb)
            rows = pl.ds(r0, rb)
            if masked:
                rchunk = (r0 + lax.broadcasted_iota(jnp.int32, (rb, tq), 0)) // CHUNK
                cchunk = lax.broadcasted_iota(jnp.int32, (rb, tq), 1) // CHUNK
                visible = rchunk >= cchunk
            for mp in range(2):
                s = s_ref[mp, rows, :]
                if masked:
                    s = jnp.where(visible, s, MASK_VALUE)
                m_old = m_ref[mp, rows, :]
                m_new = jnp.maximum(m_old, jnp.max(s, axis=1, keepdims=True))
                alpha = jnp.exp2(m_old - m_new)
                psum = None
                for c in range(n_col):
                    cols = slice(c * LANES, (c + 1) * LANES)
                    pc = jnp.exp2(s[:, cols] - m_new)
                    p_ref[mp, rows, cols] = pc.astype(BF16)
                    psum = pc if psum is None else psum + pc
                l_ref[mp, rows, :] = (alpha * l_ref[mp, rows, :]
                                      + jnp.sum(psum, axis=1, keepdims=True))
                m_ref[mp, rows, :] = m_new
                al_ref[mp, rows, :] = alpha
            return carry

        lax.fori_loop(0, tq // rb, body, 0, unroll=True)

    def step(kb, masked):
        r0 = pl.multiple_of(kb * tq, tq)
        k = k_ref[pl.ds(r0, tq), :]
        v = v_ref[pl.ds(r0, tq), :]
        s_ref[0] = lax.dot_general(q[:, :d], k[:, :d], NT_DIMS, preferred_element_type=F32)
        s_ref[1] = lax.dot_general(q[:, d:], k[:, d:], NT_DIMS, preferred_element_type=F32)
        softmax_rows(masked)
        for mp in range(2):
            pv = jnp.dot(p_ref[mp], v, preferred_element_type=F32)
            al = al_ref[mp]
            for c in range(DIFF_V_DIM // LANES):
                cols = slice(c * LANES, (c + 1) * LANES)
                acc_ref[mp, :, cols] = al * acc_ref[mp, :, cols] + pv[:, cols]

    def loop_body(kb, carry):
        step(kb, False)
        return carry

    lax.fori_loop(0, qi, loop_body, 0)
    step(qi, True)

    lp = lam_ref[...]
    lam = (jnp.exp(jnp.sum(lp[0:1] * lp[1:2], axis=1, keepdims=True))
           - jnp.exp(jnp.sum(lp[2:3] * lp[3:4], axis=1, keepdims=True)) + lam_init)
    w0 = 1.0 / l_ref[0]
    w1 = lam * (1.0 / l_ref[1])
    halves = []
    ssq = None
    for c in range(DIFF_V_DIM // LANES):
        cols = slice(c * LANES, (c + 1) * LANES)
        oc = acc_ref[0, :, cols] * w0 - acc_ref[1, :, cols] * w1
        halves.append(oc)
        sq = jnp.sum(oc * oc, axis=1, keepdims=True)
        ssq = sq if ssq is None else ssq + sq
    scale = lax.rsqrt(ssq * (1.0 / DIFF_V_DIM) + SUBLN_EPS) * (1.0 - lam_init)
    for c, oc in enumerate(halves):
        cols = slice(c * LANES, (c + 1) * LANES)
        o_ref[:, cols] = (oc * scale * sw_ref[:, cols]).astype(o_ref.dtype)


def _attn(proj, lam_all, subln_all, *, layer, lam_init, tq, rb, q_col, k_col, v_col):
    s = proj.shape[0]
    hw = 2 * DIFF_HEAD_DIM
    qb, kb, vb = q_col // hw, k_col // hw, v_col // hw
    return pl.pallas_call(
        functools.partial(_attn_kernel, lam_init=lam_init, tq=tq, rb=rb),
        grid=(DIFF_HEADS, s // tq),
        in_specs=[
            pl.BlockSpec((None,) + lam_all.shape[1:], lambda h, i: (layer, 0, 0)),
            pl.BlockSpec((None, 1, DIFF_V_DIM), lambda h, i: (layer, 0, 0)),
            pl.BlockSpec((tq, hw), lambda h, i: (i, qb + h)),
            pl.BlockSpec((s, hw), lambda h, i: (0, kb + h)),
            pl.BlockSpec((s, DIFF_V_DIM), lambda h, i: (0, vb + h)),
        ],
        out_specs=pl.BlockSpec((tq, DIFF_V_DIM), lambda h, i: (i, h)),
        out_shape=jax.ShapeDtypeStruct((s, DIFF_HEADS * DIFF_V_DIM), BF16),
        scratch_shapes=[
            pltpu.VMEM((2, tq, tq), F32),
            pltpu.VMEM((2, tq, tq), BF16),
            pltpu.VMEM((2, tq, LANES), F32),
            pltpu.VMEM((2, tq, LANES), F32),
            pltpu.VMEM((2, tq, LANES), F32),
            pltpu.VMEM((2, tq, DIFF_V_DIM), F32),
        ],
        compiler_params=_params("parallel", "arbitrary"),
        name="diff_attn",
    )(lam_all, subln_all, proj, proj, proj)


def _merge_kernel(x_ref, oa_ref, ob_ref, ga0_ref, ga1_ref, gb0_ref, gb1_ref,
                  wa_ref, wb_ref, wo_ref, nwo_ref, nwn_ref, out_ref, h_ref, mg_ref):
    half = ga0_ref.shape[1]
    oa = oa_ref[...]
    ob = ob_ref[...]
    for part, (ga_ref, gb_ref) in enumerate(((ga0_ref, gb0_ref), (ga1_ref, gb1_ref))):
        sl = slice(part * half, (part + 1) * half)
        ya = jnp.dot(oa, wa_ref[:, sl], preferred_element_type=F32)
        yb = jnp.dot(ob, wb_ref[:, sl], preferred_element_type=F32)
        merged = (jax.nn.sigmoid(ga_ref[...].astype(F32)) * ya
                  + jax.nn.sigmoid(gb_ref[...].astype(F32)) * yb)
        mg_ref[:, sl] = merged.astype(BF16)
    z = jnp.dot(mg_ref[...], wo_ref[...], preferred_element_type=F32)
    x = x_ref[...] + _rms(z, NORM_EPS) * nwo_ref[...]
    out_ref[...] = x
    h_ref[...] = (_rms(x, NORM_EPS) * nwn_ref[...]).astype(h_ref.dtype)


def _merge(x, o_a, o_b, proj, wa_all, wb_all, wo_all, nw_all, *, layer, tm, gate_col):
    s, d = x.shape
    half = d // 2
    gb = gate_col // half
    wa_k, wb_k = wa_all.shape[1], wb_all.shape[1]
    const = lambda rows: pl.BlockSpec((None, rows, d), lambda i: (layer, 0, 0),
                                      pipeline_mode=pl.Buffered(1))
    nw_row = lambda r: pl.BlockSpec((None, 1, d), lambda i, r=r: (NORMS_PER_LAYER * layer + r, 0, 0))
    gate = lambda b: pl.BlockSpec((tm, half), lambda i, b=b: (i, b))
    return pl.pallas_call(
        _merge_kernel,
        grid=(s // tm,),
        in_specs=[
            pl.BlockSpec((tm, d), lambda i: (i, 0)),
            pl.BlockSpec((tm, wa_k), lambda i: (i, 0)),
            pl.BlockSpec((tm, wb_k), lambda i: (i, 0)),
            gate(gb), gate(gb + 1), gate(gb + 2), gate(gb + 3),
            const(wa_k), const(wb_k), const(d), nw_row(1), nw_row(2),
        ],
        out_specs=[pl.BlockSpec((tm, d), lambda i: (i, 0)), pl.BlockSpec((tm, d), lambda i: (i, 0))],
        out_shape=[jax.ShapeDtypeStruct((s, d), F32), jax.ShapeDtypeStruct((s, d), BF16)],
        scratch_shapes=[pltpu.VMEM((tm, d), BF16)],
        compiler_params=_params("parallel"),
        name="merge",
    )(x, o_a, o_b, proj, proj, proj, proj, wa_all, wb_all, wo_all, nw_all, nw_all)


def _ffn_up_kernel(h_ref, wg_ref, wu_ref, a_ref, wgb_ref, wub_ref):
    @pl.when(pl.program_id(1) == 0)
    def _():
        wgb_ref[...] = wg_ref[...].astype(BF16)
        wub_ref[...] = wu_ref[...].astype(BF16)

    h = h_ref[...]
    g = jnp.dot(h, wgb_ref[...], preferred_element_type=F32)
    u = jnp.dot(h, wub_ref[...], preferred_element_type=F32)
    a_ref[...] = (_silu(g) * u).astype(a_ref.dtype)


def _ffn_up(h, w_all, *, layer, tm, th):
    s, d = h.shape
    hidden = w_all.shape[2] // 2
    nh = hidden // th
    return pl.pallas_call(
        _ffn_up_kernel,
        grid=(nh, s // tm),
        in_specs=[
            pl.BlockSpec((tm, d), lambda j, i: (i, 0)),
            pl.BlockSpec((None, d, th), lambda j, i: (layer, 0, j)),
            pl.BlockSpec((None, d, th), lambda j, i: (layer, 0, nh + j)),
        ],
        out_specs=pl.BlockSpec((tm, th), lambda j, i: (i, j)),
        out_shape=jax.ShapeDtypeStruct((s, hidden), BF16),
        scratch_shapes=[pltpu.VMEM((d, th), BF16), pltpu.VMEM((d, th), BF16)],
        compiler_params=_params("arbitrary", "arbitrary"),
        name="ffn_up",
    )(h, w_all, w_all)


def _ffn_down_kernel(x_ref, a_ref, wd_ref, nwo_ref, *rest, emit_next):
    z = jnp.dot(a_ref[...], wd_ref[...], preferred_element_type=F32)
    x = x_ref[...] + _rms(z, NORM_EPS) * nwo_ref[...]
    if emit_next:
        nwn_ref, out_ref, h_ref = rest
        h_ref[...] = (_rms(x, NORM_EPS) * nwn_ref[...]).astype(h_ref.dtype)
    else:
        (out_ref,) = rest
    out_ref[...] = x


def _ffn_down(x, a, wd_all, nw_all, *, layer, tm):
    s, d = x.shape
    hidden = a.shape[1]
    emit_next = NORMS_PER_LAYER * (layer + 1) < nw_all.shape[0]
    row = pl.BlockSpec((tm, d), lambda i: (i, 0))
    in_specs = [
        row,
        pl.BlockSpec((tm, hidden), lambda i: (i, 0)),
        pl.BlockSpec((None, hidden, d), lambda i: (layer, 0, 0), pipeline_mode=pl.Buffered(1)),
        pl.BlockSpec((None, 1, d), lambda i: (NORMS_PER_LAYER * layer + 3, 0, 0)),
    ]
    args = [x, a, wd_all, nw_all]
    out_specs, out_shape = [row], [jax.ShapeDtypeStruct((s, d), F32)]
    if emit_next:
        in_specs.append(pl.BlockSpec((None, 1, d), lambda i: (NORMS_PER_LAYER * (layer + 1), 0, 0)))
        args.append(nw_all)
        out_specs.append(row)
        out_shape.append(jax.ShapeDtypeStruct((s, d), BF16))
    outs = pl.pallas_call(
        functools.partial(_ffn_down_kernel, emit_next=emit_next),
        grid=(s // tm,),
        in_specs=in_specs,
        out_specs=out_specs,
        out_shape=out_shape,
        compiler_params=_params("parallel"),
        name="ffn_down",
    )(*args)
    return (outs[0], outs[1]) if emit_next else (outs[0], None)


def _rope_tables(seq):
    half = ROPE_DIM // 2
    pos = jnp.arange(seq, dtype=F32)
    inv_freq = ROPE_THETA ** (-jnp.arange(0, ROPE_DIM, 2, dtype=F32) / ROPE_DIM)
    ang = pos[:, None] * inv_freq[None, :]
    cos, sin = jnp.cos(ang), jnp.sin(ang)
    zeros = jnp.zeros((seq, LANES - ROPE_DIM), F32)
    zh = jnp.zeros((seq, half), F32)
    c = jnp.concatenate([cos, cos, jnp.ones_like(zeros)], axis=1)
    s_dn = jnp.concatenate([-sin, zh, zeros], axis=1)
    s_up = jnp.concatenate([zh, sin, zeros], axis=1)
    k_tab = jnp.stack([c, s_dn, s_up])
    return jnp.stack([k_tab * (DIFF_HEAD_DIM ** -0.5 * math.log2(math.e)), k_tab])


def kernel(x, w_in, hgrn_lower_bounds, hgrn_gnorm_w, diff_lambda, diff_subln_w,
           w_branch_a, w_branch_b, w_out, norm_w, w_ffn_in, w_ffn_out):
    batch, seq, d_model = x.shape
    assert batch == 1
    depth = w_in.shape[0]
    q_col = 4 * HGRN_WIDTH
    k_col = q_col + DIFF_HEADS * 2 * DIFF_HEAD_DIM
    v_col = k_col + DIFF_HEADS * 2 * DIFF_HEAD_DIM
    gate_col = v_col + DIFF_HEADS * DIFF_V_DIM
    tile = {name: min(t, seq) for name, t in dict(
        norm=512, in_proj=1024, hgrn=512, attn=512, merge=256, ffn_up=1024, ffn_down=256).items()}
    tab = _rope_tables(seq)
    xs = x[0]
    assert norm_w.shape[1] == NORMS_PER_LAYER
    nw_all = norm_w.astype(F32).reshape(depth * NORMS_PER_LAYER, 1, d_model)
    lb_raw = hgrn_lower_bounds.astype(F32)
    gnorm = hgrn_gnorm_w.astype(F32)
    lam_all = diff_lambda.astype(F32)
    subln_all = diff_subln_w.astype(F32)[:, None, :]
    wa16, wb16, wo16 = w_branch_a.astype(BF16), w_branch_b.astype(BF16), w_out.astype(BF16)
    wd16 = w_ffn_out.astype(BF16)
    h = _norm(xs, nw_all[0], tm=tile["norm"])
    for l in range(depth):
        lam_init = 0.8 - 0.6 * math.exp(-0.3 * l)
        proj = _in_proj(h, w_in, tab, layer=l, tm=tile["in_proj"], tn=1024,
                        q_col=q_col, k_col=k_col, v_col=v_col)
        o_a = _hgrn(proj, lb_raw, gnorm[l:l + 1], layer=l, tt=tile["hgrn"])
        o_b = _attn(proj, lam_all, subln_all, layer=l, lam_init=lam_init, tq=tile["attn"], rb=32,
                    q_col=q_col, k_col=k_col, v_col=v_col)
        xs, hf = _merge(xs, o_a, o_b, proj, wa16, wb16, wo16, nw_all,
                        layer=l, tm=tile["merge"], gate_col=gate_col)
        a = _ffn_up(hf, w_ffn_in, layer=l, tm=tile["ffn_up"], th=512)
        xs, h = _ffn_down(xs, a, wd16, nw_all, layer=l, tm=tile["ffn_down"])
    return xs[None]
```

```python
import functools
import math

import jax
import jax.numpy as jnp
from jax import lax
from jax.experimental import pallas as pl
from jax.experimental.pallas import tpu as pltpu

F32 = jnp.float32
BF16 = jnp.bfloat16

CHUNK = 64
HGRN_HEADS = 8
HGRN_DIM = 128
HGRN_WIDTH = HGRN_HEADS * HGRN_DIM
DIFF_HEADS = 4
DIFF_HEAD_DIM = 128
DIFF_V_DIM = 2 * DIFF_HEAD_DIM
ROPE_THETA = 500000.0
ROPE_DIM = DIFF_HEAD_DIM // 4
NORM_EPS = 1e-6
SUBLN_EPS = 1e-5
LANES = 128
SUBLANES = 8

HGRN_CHUNK = 64
HGRN_FAST_MAX_DECAY = 80.0
MASK_VALUE = -1e30
NORMS_PER_LAYER = 4

VMEM_LIMIT = 56 * 1024 * 1024

NT_DIMS = (((1,), (1,)), ((), ()))
TN_DIMS = (((0,), (0,)), ((), ()))


def _rms(x, eps):
    return x * lax.rsqrt(jnp.mean(x * x, axis=-1, keepdims=True) + eps)


def _silu(x):
    return x * jax.nn.sigmoid(x)


def _params(*sem):
    return pltpu.CompilerParams(dimension_semantics=sem, vmem_limit_bytes=VMEM_LIMIT)


def _norm_kernel(x_ref, nw_ref, h_ref):
    h_ref[...] = (_rms(x_ref[...], NORM_EPS) * nw_ref[...]).astype(h_ref.dtype)


def _norm(x, nw, *, tm):
    s, d = x.shape
    return pl.pallas_call(
        _norm_kernel,
        grid=(s // tm,),
        in_specs=[pl.BlockSpec((tm, d), lambda i: (i, 0)), pl.BlockSpec((1, d), lambda i: (0, 0))],
        out_specs=pl.BlockSpec((tm, d), lambda i: (i, 0)),
        out_shape=jax.ShapeDtypeStruct((s, d), BF16),
        compiler_params=_params("parallel"),
        name="pre_norm",
    )(x, nw)


def _inproj_kernel(h_ref, w_ref, tab_ref, o_ref, wb_ref, *, rope_lo, rope_hi):
    j = pl.program_id(0)

    @pl.when(pl.program_id(1) == 0)
    def _():
        wb_ref[...] = w_ref[...].astype(BF16)

    is_rope = jnp.logical_and(j >= rope_lo, j < rope_hi)

    @pl.when(is_rope)
    def _():
        acc = jnp.dot(h_ref[...], wb_ref[...], preferred_element_type=F32)
        c = tab_ref[0, 0]
        s_dn = tab_ref[0, 1]
        s_up = tab_ref[0, 2]
        for g in range(acc.shape[1] // LANES):
            t = acc[:, g * LANES:(g + 1) * LANES]
            r = (t * c + pltpu.roll(t, LANES - ROPE_DIM // 2, 1) * s_dn
                 + pltpu.roll(t, ROPE_DIM // 2, 1) * s_up)
            o_ref[:, g * LANES:(g + 1) * LANES] = r.astype(o_ref.dtype)

    @pl.when(jnp.logical_not(is_rope))
    def _():
        o_ref[...] = jnp.dot(h_ref[...], wb_ref[...],
                             preferred_element_type=F32).astype(o_ref.dtype)


def _in_proj(h, w_all, tab, *, layer, tm, tn, q_col, k_col, v_col):
    s, d = h.shape
    n = w_all.shape[2]
    assert q_col % tn == 0 and k_col - q_col == tn and v_col - k_col == tn
    q_blk = q_col // tn

    def tab_map(j, i):
        is_rope = jnp.logical_and(j >= q_blk, j < q_blk + 2)
        return (jnp.clip(j - q_blk, 0, 1), 0, jnp.where(is_rope, i, 0), 0)

    return pl.pallas_call(
        functools.partial(_inproj_kernel, rope_lo=q_blk, rope_hi=q_blk + 2),
        grid=(n // tn, s // tm),
        in_specs=[
            pl.BlockSpec((tm, d), lambda j, i: (i, 0)),
            pl.BlockSpec((None, d, tn), lambda j, i: (layer, 0, j)),
            pl.BlockSpec((1, 3, tm, LANES), tab_map),
        ],
        out_specs=pl.BlockSpec((tm, tn), lambda j, i: (i, j)),
        out_shape=jax.ShapeDtypeStruct((s, n), BF16),
        scratch_shapes=[pltpu.VMEM((d, tn), BF16)],
        compiler_params=_params("arbitrary", "arbitrary"),
        name="in_proj",
    )(h, w_all, tab)


def _chunk_cumsum(x):
    groups = x.shape[0] // SUBLANES
    x3 = x.reshape(groups, SUBLANES, x.shape[1])
    sub = lax.broadcasted_iota(jnp.int32, (1, SUBLANES, x.shape[1]), 1)
    for shift in (1, 2, 4):
        x3 = x3 + jnp.where(sub >= shift, pltpu.roll(x3, shift, 1), 0.0)
    tot = x3[:, SUBLANES - 1:SUBLANES, :]
    offs = [jnp.zeros_like(tot[0:1])]
    for g in range(1, groups):
        offs.append(offs[-1] + tot[g - 1:g])
    return (x3 + jnp.concatenate(offs, axis=0)).reshape(x.shape)


def _hgrn_kernel(lbraw_ref, gw_ref, q_ref, f_ref, i_ref, g_ref, o_ref,
                 state_ref, lbp_ref, qg_ref, kinv_ref, kdec_ref, eg_ref,
                 osc_ref, gsc_ref, ksc_ref, vsc_ref, *, layer, tt):
    c_len = HGRN_CHUNK
    n_chunks = tt // c_len
    dh = HGRN_DIM
    width = HGRN_WIDTH

    @pl.when(pl.program_id(0) == 0)
    def _():
        state_ref[...] = jnp.zeros_like(state_ref)
        gsc_ref[...] = jnp.zeros_like(gsc_ref)
        ksc_ref[...] = jnp.zeros_like(ksc_ref)
        vsc_ref[...] = jnp.zeros_like(vsc_ref)
        raw = lbraw_ref[...]
        depth = raw.shape[0]
        mx = raw[0:1]
        for r in range(1, depth):
            mx = jnp.maximum(mx, raw[r:r + 1])
        es = [jnp.exp(raw[r:r + 1] - mx) for r in range(depth)]
        tot = es[0]
        for r in range(1, depth):
            tot = tot + es[r]
        lb = jnp.zeros_like(tot)
        for r in range(1, layer + 1):
            lb = lb + es[r] / tot
        lbp_ref[...] = jnp.maximum(lb, jnp.finfo(F32).tiny)

    causal = (lax.broadcasted_iota(jnp.int32, (c_len, c_len), 0)
              >= lax.broadcasted_iota(jnp.int32, (c_len, c_len), 1))
    lbp = lbp_ref[...]
    gw = gw_ref[...]

    def chunk_rows(c):
        return pl.ds(pl.multiple_of(c * c_len, c_len), c_len)

    def gates(rows):
        qr = q_ref[rows, :].astype(F32)
        fr = f_ref[rows, :].astype(F32)
        t = jnp.exp(-jnp.abs(fr))
        pos = fr >= 0.0
        inv = 1.0 / (1.0 + t)
        f = jnp.where(pos, 1.0 + lbp * t, t + lbp) * inv
        kk = jnp.where(pos, t, 1.0) * inv * (1.0 - lbp)
        return _silu(qr), kk, _chunk_cumsum(jnp.log2(f))

    def prepare(c, gmin):
        rows = chunk_rows(c)
        q, kk, gcum = gates(rows)
        glast = gcum[c_len - 1:c_len, :]
        qg_ref[rows, :] = (q * jnp.exp2(gcum)).astype(BF16)
        kinv_ref[rows, :] = (kk * jnp.exp2(-gcum)).astype(BF16)
        kdec_ref[rows, :] = (kk * jnp.exp2(glast - gcum)).astype(BF16)
        eg_ref[c] = jnp.exp2(glast)
        return jnp.minimum(gmin, glast)

    gmin = lax.fori_loop(0, n_chunks, prepare, jnp.zeros((1, width), F32))
    fast = jnp.min(gmin) > -HGRN_FAST_MAX_DECAY * math.log2(math.e)

    heads = [slice(h * dh, (h + 1) * dh) for h in range(HGRN_HEADS)]

    def finish_heads(c, rows, o_intra):
        states = [state_ref[h] for h in range(HGRN_HEADS)]
        o_inter = [lax.dot_general(qg_ref[rows, sl], st.astype(BF16), NT_DIMS,
                                   preferred_element_type=F32) for sl, st in zip(heads, states)]
        upd = [lax.dot_general(i_ref[rows, sl], kdec_ref[rows, sl], TN_DIMS,
                               preferred_element_type=F32) for sl in heads]
        for h, sl in enumerate(heads):
            state_ref[h] = states[h] * eg_ref[c, :, sl] + upd[h]
        for h, sl in enumerate(heads):
            gate = _silu(g_ref[rows, sl].astype(F32))
            o = o_intra[h] + o_inter[h]
            o_ref[rows, sl] = (_rms(o, NORM_EPS) * gw * gate).astype(o_ref.dtype)

    def fast_chunk(c, carry):
        rows = chunk_rows(c)
        a = [lax.dot_general(qg_ref[rows, sl], kinv_ref[rows, sl], NT_DIMS,
                             preferred_element_type=F32) for sl in heads]
        a = [jnp.where(causal, ah, 0.0).astype(BF16) for ah in a]
        o_intra = [jnp.dot(ah, i_ref[rows, sl], preferred_element_type=F32)
                   for ah, sl in zip(a, heads)]
        finish_heads(c, rows, o_intra)
        return carry

    def slow_chunk(c, carry):
        rows = chunk_rows(c)
        q, kk, gcum = gates(rows)
        gsc_ref[c_len:, :] = gcum
        ksc_ref[c_len:, :] = kk
        vsc_ref[c_len:, :] = i_ref[rows, :].astype(F32)
        osc_ref[...] = jnp.zeros_like(osc_ref)

        def off_body(d, carry2):
            gs = pltpu.roll(gsc_ref[...], d, 0)[c_len:, :]
            ks = pltpu.roll(ksc_ref[...], d, 0)[c_len:, :]
            vs = pltpu.roll(vsc_ref[...], d, 0)[c_len:, :]
            p = q * ks * jnp.exp2(jnp.minimum(gcum - gs, 0.0))
            for h in range(HGRN_HEADS):
                sl = slice(h * dh, (h + 1) * dh)
                osc_ref[:, sl] += jnp.sum(p[:, sl], axis=1, keepdims=True) * vs[:, sl]
            return carry2

        lax.fori_loop(0, c_len, off_body, 0)
        finish_heads(c, rows, [osc_ref[:, sl] for sl in heads])
        return carry

    @pl.when(fast)
    def _():
        lax.fori_loop(0, n_chunks, fast_chunk, 0, unroll=2)

    @pl.when(jnp.logical_not(fast))
    def _():
        lax.fori_loop(0, n_chunks, slow_chunk, 0)


def _hgrn(proj, lb_raw, gnorm_w, *, layer, tt):
    s = proj.shape[0]
    w = HGRN_WIDTH
    depth = lb_raw.shape[0]
    col_spec = lambda b: pl.BlockSpec((tt, w), lambda i, b=b: (i, b))
    return pl.pallas_call(
        functools.partial(_hgrn_kernel, layer=layer, tt=tt),
        grid=(s // tt,),
        in_specs=[
            pl.BlockSpec((depth, w), lambda i: (0, 0)),
            pl.BlockSpec((1, HGRN_DIM), lambda i: (0, 0)),
            col_spec(0), col_spec(1), col_spec(2), col_spec(3),
        ],
        out_specs=pl.BlockSpec((tt, w), lambda i: (i, 0)),
        out_shape=jax.ShapeDtypeStruct((s, w), BF16),
        scratch_shapes=[
            pltpu.VMEM((HGRN_HEADS, HGRN_DIM, HGRN_DIM), F32),
            pltpu.VMEM((1, w), F32),
            pltpu.VMEM((tt, w), BF16),
            pltpu.VMEM((tt, w), BF16),
            pltpu.VMEM((tt, w), BF16),
            pltpu.VMEM((tt // HGRN_CHUNK, 1, w), F32),
            pltpu.VMEM((HGRN_CHUNK, w), F32),
            pltpu.VMEM((2 * HGRN_CHUNK, w), F32),
            pltpu.VMEM((2 * HGRN_CHUNK, w), F32),
            pltpu.VMEM((2 * HGRN_CHUNK, w), F32),
        ],
        compiler_params=_params("arbitrary"),
        name="hgrn",
    )(lb_raw, gnorm_w, proj, proj, proj, proj)


def _attn_kernel(lam_ref, sw_ref, q_ref, k_ref, v_ref, o_ref,
                 s0_ref, s1_ref, p0_ref, p1_ref, al0_ref, al1_ref, m_ref, l_ref, acc_ref,
                 *, lam_init, tq, rb):
    s_refs, p_refs, al_refs = (s0_ref, s1_ref), (p0_ref, p1_ref), (al0_ref, al1_ref)
    qi = pl.program_id(1)
    d = DIFF_HEAD_DIM
    n_col = tq // LANES
    m_ref[...] = jnp.full_like(m_ref, MASK_VALUE)
    l_ref[...] = jnp.zeros_like(l_ref)
    acc_ref[...] = jnp.zeros_like(acc_ref)

    def softmax_rows(masked, s_ref, p_ref, al_ref):
        def load_scores(mp, r0):
            s = s_ref[mp, pl.ds(r0, rb), :]
            if masked:
                rchunk = (r0 + lax.broadcasted_iota(jnp.int32, (rb, tq), 0)) // CHUNK
                cchunk = lax.broadcasted_iota(jnp.int32, (rb, tq), 1) // CHUNK
                s = jnp.where(rchunk >= cchunk, s, MASK_VALUE)
            return s

        def max_body(r, carry):
            r0 = pl.multiple_of(r * rb, rb)
            rows = pl.ds(r0, rb)
            for mp in range(2):
                m_old = m_ref[mp, rows, :]
                m_new = jnp.maximum(m_old, jnp.max(load_scores(mp, r0), axis=1, keepdims=True))
                al_ref[mp, rows, :] = jnp.exp2(m_old - m_new)
                m_ref[mp, rows, :] = m_new
            return carry

        def exp_body(r, carry):
            r0 = pl.multiple_of(r * rb, rb)
            rows = pl.ds(r0, rb)
            for mp in range(2):
                s = load_scores(mp, r0)
                m_new = m_ref[mp, rows, :]
                psum = None
                for c in range(n_col):
                    cols = slice(c * LANES, (c + 1) * LANES)
                    pc = jnp.exp2(s[:, cols] - m_new)
                    p_ref[mp, rows, cols] = pc.astype(BF16)
                    psum = pc if psum is None else psum + pc
                l_ref[mp, rows, :] = (al_ref[mp, rows, :] * l_ref[mp, rows, :]
                                      + jnp.sum(psum, axis=1, keepdims=True))
            return carry

        lax.fori_loop(0, tq // rb, max_body, 0, unroll=True)
        lax.fori_loop(0, tq // rb, exp_body, 0, unroll=True)

    def scores(kb, s_ref):
        k = k_ref[pl.ds(pl.multiple_of(kb * tq, tq), tq), :]
        s_ref[0] = lax.dot_general(q_ref[:, :d], k[:, :d], NT_DIMS, preferred_element_type=F32)
        s_ref[1] = lax.dot_general(q_ref[:, d:], k[:, d:], NT_DIMS, preferred_element_type=F32)

    def accumulate(kb, p_ref, al_ref):
        v = v_ref[pl.ds(pl.multiple_of(kb * tq, tq), tq), :]
        for mp in range(2):
            pv = jnp.dot(p_ref[mp], v, preferred_element_type=F32)
            al = al_ref[mp]
            for c in range(DIFF_V_DIM // LANES):
                cols = slice(c * LANES, (c + 1) * LANES)
                acc_ref[mp, :, cols] = al * acc_ref[mp, :, cols] + pv[:, cols]

    def trip(kb, cur):
        oth = 1 - cur
        scores(kb + 1, s_refs[oth])
        accumulate(jnp.maximum(kb - 1, 0), p_refs[oth], al_refs[oth])
        softmax_rows(False, s_refs[cur], p_refs[cur], al_refs[cur])

    def drain(cur):
        oth = 1 - cur
        accumulate(jnp.maximum(qi - 1, 0), p_refs[oth], al_refs[oth])
        softmax_rows(True, s_refs[cur], p_refs[cur], al_refs[cur])
        accumulate(qi, p_refs[cur], al_refs[cur])

    p_refs[1][...] = jnp.zeros_like(p_refs[1])
    al_refs[1][...] = jnp.zeros_like(al_refs[1])
    scores(0, s_refs[0])

    def trip_body(kb, carry):
        even = kb % 2 == 0

        @pl.when(even)
        def _():
            trip(kb, 0)

        @pl.when(jnp.logical_not(even))
        def _():
            trip(kb, 1)

        return carry

    lax.fori_loop(0, qi, trip_body, 0)
    odd = qi % 2 == 1

    @pl.when(odd)
    def _():
        drain(1)

    @pl.when(jnp.logical_not(odd))
    def _():
        drain(0)

    lp = lam_ref[...]
    lam = (jnp.exp(jnp.sum(lp[0:1] * lp[1:2], axis=1, keepdims=True))
           - jnp.exp(jnp.sum(lp[2:3] * lp[3:4], axis=1, keepdims=True)) + lam_init)
    w0 = 1.0 / l_ref[0]
    w1 = lam * (1.0 / l_ref[1])
    halves = []
    ssq = None
    for c in range(DIFF_V_DIM // LANES):
        cols = slice(c * LANES, (c + 1) * LANES)
        oc = acc_ref[0, :, cols] * w0 - acc_ref[1, :, cols] * w1
        halves.append(oc)
        sq = jnp.sum(oc * oc, axis=1, keepdims=True)
        ssq = sq if ssq is None else ssq + sq
    scale = lax.rsqrt(ssq * (1.0 / DIFF_V_DIM) + SUBLN_EPS) * (1.0 - lam_init)
    for c, oc in enumerate(halves):
        cols = slice(c * LANES, (c + 1) * LANES)
        o_ref[:, cols] = (oc * scale * sw_ref[:, cols]).astype(o_ref.dtype)


def _attn(proj, lam_all, subln_all, *, layer, lam_init, tq, rb, q_col, k_col, v_col):
    s = proj.shape[0]
    hw = 2 * DIFF_HEAD_DIM
    qb, kb, vb = q_col // hw, k_col // hw, v_col // hw
    return pl.pallas_call(
        functools.partial(_attn_kernel, lam_init=lam_init, tq=tq, rb=rb),
        grid=(DIFF_HEADS, s // tq),
        in_specs=[
            pl.BlockSpec((None,) + lam_all.shape[1:], lambda h, i: (layer, 0, 0)),
            pl.BlockSpec((None, 1, DIFF_V_DIM), lambda h, i: (layer, 0, 0)),
            pl.BlockSpec((tq, hw), lambda h, i: (i, qb + h)),
            pl.BlockSpec((s, hw), lambda h, i: (0, kb + h)),
            pl.BlockSpec((s, DIFF_V_DIM), lambda h, i: (0, vb + h)),
        ],
        out_specs=pl.BlockSpec((tq, DIFF_V_DIM), lambda h, i: (i, h)),
        out_shape=jax.ShapeDtypeStruct((s, DIFF_HEADS * DIFF_V_DIM), BF16),
        scratch_shapes=[
            pltpu.VMEM((2, tq, tq), F32),
            pltpu.VMEM((2, tq, tq), F32),
            pltpu.VMEM((2, tq, tq), BF16),
            pltpu.VMEM((2, tq, tq), BF16),
            pltpu.VMEM((2, tq, LANES), F32),
            pltpu.VMEM((2, tq, LANES), F32),
            pltpu.VMEM((2, tq, LANES), F32),
            pltpu.VMEM((2, tq, LANES), F32),
            pltpu.VMEM((2, tq, DIFF_V_DIM), F32),
        ],
        compiler_params=_params("parallel", "arbitrary"),
        name="diff_attn",
    )(lam_all, subln_all, proj, proj, proj)


def _merge_kernel(x_ref, oa_ref, ob_ref, ga0_ref, ga1_ref, gb0_ref, gb1_ref,
                  wa_ref, wb_ref, wo_ref, nwo_ref, nwn_ref, out_ref, h_ref, mg_ref):
    half = ga0_ref.shape[1]
    oa = oa_ref[...]
    ob = ob_ref[...]
    for part, (ga_ref, gb_ref) in enumerate(((ga0_ref, gb0_ref), (ga1_ref, gb1_ref))):
        sl = slice(part * half, (part + 1) * half)
        ya = jnp.dot(oa, wa_ref[:, sl], preferred_element_type=F32)
        yb = jnp.dot(ob, wb_ref[:, sl], preferred_element_type=F32)
        merged = (jax.nn.sigmoid(ga_ref[...].astype(F32)) * ya
                  + jax.nn.sigmoid(gb_ref[...].astype(F32)) * yb)
        mg_ref[:, sl] = merged.astype(BF16)
    z = jnp.dot(mg_ref[...], wo_ref[...], preferred_element_type=F32)
    x = x_ref[...] + _rms(z, NORM_EPS) * nwo_ref[...]
    out_ref[...] = x
    h_ref[...] = (_rms(x, NORM_EPS) * nwn_ref[...]).astype(h_ref.dtype)


def _merge(x, o_a, o_b, proj, wa_all, wb_all, wo_all, nw_all, *, layer, tm, gate_col):
    s, d = x.shape
    half = d // 2
    gb = gate_col // half
    wa_k, wb_k = wa_all.shape[1], wb_all.shape[1]
    const = lambda rows: pl.BlockSpec((None, rows, d), lambda i: (layer, 0, 0),
                                      pipeline_mode=pl.Buffered(1))
    nw_row = lambda r: pl.BlockSpec((None, 1, d), lambda i, r=r: (NORMS_PER_LAYER * layer + r, 0, 0))
    gate = lambda b: pl.BlockSpec((tm, half), lambda i, b=b: (i, b))
    return pl.pallas_call(
        _merge_kernel,
        grid=(s // tm,),
        in_specs=[
            pl.BlockSpec((tm, d), lambda i: (i, 0)),
            pl.BlockSpec((tm, wa_k), lambda i: (i, 0)),
            pl.BlockSpec((tm, wb_k), lambda i: (i, 0)),
            gate(gb), gate(gb + 1), gate(gb + 2), gate(gb + 3),
            const(wa_k), const(wb_k), const(d), nw_row(1), nw_row(2),
        ],
        out_specs=[pl.BlockSpec((tm, d), lambda i: (i, 0)), pl.BlockSpec((tm, d), lambda i: (i, 0))],
        out_shape=[jax.ShapeDtypeStruct((s, d), F32), jax.ShapeDtypeStruct((s, d), BF16)],
        scratch_shapes=[pltpu.VMEM((tm, d), BF16)],
        compiler_params=_params("parallel"),
        name="merge",
    )(x, o_a, o_b, proj, proj, proj, proj, wa_all, wb_all, wo_all, nw_all, nw_all)


def _ffn_up_kernel(h_ref, wg_ref, wu_ref, a_ref, wgb_ref, wub_ref):
    @pl.when(pl.program_id(1) == 0)
    def _():
        wgb_ref[...] = wg_ref[...].astype(BF16)
        wub_ref[...] = wu_ref[...].astype(BF16)

    h = h_ref[...]
    g = jnp.dot(h, wgb_ref[...], preferred_element_type=F32)
    u = jnp.dot(h, wub_ref[...], preferred_element_type=F32)
    a_ref[...] = (_silu(g) * u).astype(a_ref.dtype)


def _ffn_up(h, w_all, *, layer, tm, th):
    s, d = h.shape
    hidden = w_all.shape[2] // 2
    nh = hidden // th
    return pl.pallas_call(
        _ffn_up_kernel,
        grid=(nh, s // tm),
        in_specs=[
            pl.BlockSpec((tm, d), lambda j, i: (i, 0)),
            pl.BlockSpec((None, d, th), lambda j, i: (layer, 0, j)),
            pl.BlockSpec((None, d, th), lambda j, i: (layer, 0, nh + j)),
        ],
        out_specs=pl.BlockSpec((tm, th), lambda j, i: (i, j)),
        out_shape=jax.ShapeDtypeStruct((s, hidden), BF16),
        scratch_shapes=[pltpu.VMEM((d, th), BF16), pltpu.VMEM((d, th), BF16)],
        compiler_params=_params("arbitrary", "arbitrary"),
        name="ffn_up",
    )(h, w_all, w_all)


def _ffn_down_kernel(x_ref, a_ref, wd_ref, nwo_ref, *rest, emit_next):
    z = jnp.dot(a_ref[...], wd_ref[...], preferred_element_type=F32)
    x = x_ref[...] + _rms(z, NORM_EPS) * nwo_ref[...]
    if emit_next:
        nwn_ref, out_ref, h_ref = rest
        h_ref[...] = (_rms(x, NORM_EPS) * nwn_ref[...]).astype(h_ref.dtype)
    else:
        (out_ref,) = rest
    out_ref[...] = x


def _ffn_down(x, a, wd_all, nw_all, *, layer, tm):
    s, d = x.shape
    hidden = a.shape[1]
    emit_next = NORMS_PER_LAYER * (layer + 1) < nw_all.shape[0]
    row = pl.BlockSpec((tm, d), lambda i: (i, 0))
    in_specs = [
        row,
        pl.BlockSpec((tm, hidden), lambda i: (i, 0)),
        pl.BlockSpec((None, hidden, d), lambda i: (layer, 0, 0), pipeline_mode=pl.Buffered(1)),
        pl.BlockSpec((None, 1, d), lambda i: (NORMS_PER_LAYER * layer + 3, 0, 0)),
    ]
    args = [x, a, wd_all, nw_all]
    out_specs, out_shape = [row], [jax.ShapeDtypeStruct((s, d), F32)]
    if emit_next:
        in_specs.append(pl.BlockSpec((None, 1, d), lambda i: (NORMS_PER_LAYER * (layer + 1), 0, 0)))
        args.append(nw_all)
        out_specs.append(row)
        out_shape.append(jax.ShapeDtypeStruct((s, d), BF16))
    outs = pl.pallas_call(
        functools.partial(_ffn_down_kernel, emit_next=emit_next),
        grid=(s // tm,),
        in_specs=in_specs,
        out_specs=out_specs,
        out_shape=out_shape,
        compiler_params=_params("parallel"),
        name="ffn_down",
    )(*args)
    return (outs[0], outs[1]) if emit_next else (outs[0], None)


def _rope_tables(seq):
    half = ROPE_DIM // 2
    pos = jnp.arange(seq, dtype=F32)
    inv_freq = ROPE_THETA ** (-jnp.arange(0, ROPE_DIM, 2, dtype=F32) / ROPE_DIM)
    ang = pos[:, None] * inv_freq[None, :]
    cos, sin = jnp.cos(ang), jnp.sin(ang)
    zeros = jnp.zeros((seq, LANES - ROPE_DIM), F32)
    zh = jnp.zeros((seq, half), F32)
    c = jnp.concatenate([cos, cos, jnp.ones_like(zeros)], axis=1)
    s_dn = jnp.concatenate([-sin, zh, zeros], axis=1)
    s_up = jnp.concatenate([zh, sin, zeros], axis=1)
    k_tab = jnp.stack([c, s_dn, s_up])
    return jnp.stack([k_tab * (DIFF_HEAD_DIM ** -0.5 * math.log2(math.e)), k_tab])


def kernel(x, w_in, hgrn_lower_bounds, hgrn_gnorm_w, diff_lambda, diff_subln_w,
           w_branch_a, w_branch_b, w_out, norm_w, w_ffn_in, w_ffn_out):
    batch, seq, d_model = x.shape
    assert batch == 1
    depth = w_in.shape[0]
    q_col = 4 * HGRN_WIDTH
    k_col = q_col + DIFF_HEADS * 2 * DIFF_HEAD_DIM
    v_col = k_col + DIFF_HEADS * 2 * DIFF_HEAD_DIM
    gate_col = v_col + DIFF_HEADS * DIFF_V_DIM
    tile = {name: min(t, seq) for name, t in dict(
        norm=512, in_proj=1024, hgrn=512, attn=512, merge=512, ffn_up=1024, ffn_down=256).items()}
    tab = _rope_tables(seq)
    xs = x[0]
    assert norm_w.shape[1] == NORMS_PER_LAYER
    nw_all = norm_w.astype(F32).reshape(depth * NORMS_PER_LAYER, 1, d_model)
    lb_raw = hgrn_lower_bounds.astype(F32)
    gnorm = hgrn_gnorm_w.astype(F32)
    lam_all = diff_lambda.astype(F32)
    subln_all = diff_subln_w.astype(F32)[:, None, :]
    wa16, wb16, wo16 = w_branch_a.astype(BF16), w_branch_b.astype(BF16), w_out.astype(BF16)
    wd16 = w_ffn_out.astype(BF16)
    h = _norm(xs, nw_all[0], tm=tile["norm"])
    for l in range(depth):
        lam_init = 0.8 - 0.6 * math.exp(-0.3 * l)
        proj = _in_proj(h, w_in, tab, layer=l, tm=tile["in_proj"], tn=1024,
                        q_col=q_col, k_col=k_col, v_col=v_col)
        o_a = _hgrn(proj, lb_raw, gnorm[l:l + 1], layer=l, tt=tile["hgrn"])
        o_b = _attn(proj, lam_all, subln_all, layer=l, lam_init=lam_init, tq=tile["attn"], rb=16,
                    q_col=q_col, k_col=k_col, v_col=v_col)
        xs, hf = _merge(xs, o_a, o_b, proj, wa16, wb16, wo16, nw_all,
                        layer=l, tm=tile["merge"], gate_col=gate_col)
        a = _ffn_up(hf, w_ffn_in, layer=l, tm=tile["ffn_up"], th=512)
        xs, h = _ffn_down(xs, a, wd16, nw_all, layer=l, tm=tile["ffn_down"])
    return xs[None]
```

```python
import functools
import math

import jax
import jax.numpy as jnp
from jax import lax
from jax.experimental import pallas as pl
from jax.experimental.pallas import tpu as pltpu

F32 = jnp.float32
BF16 = jnp.bfloat16

CHUNK = 64
HGRN_HEADS = 8
HGRN_DIM = 128
HGRN_WIDTH = HGRN_HEADS * HGRN_DIM
DIFF_HEADS = 4
DIFF_HEAD_DIM = 128
DIFF_V_DIM = 2 * DIFF_HEAD_DIM
ROPE_THETA = 500000.0
ROPE_DIM = DIFF_HEAD_DIM // 4
NORM_EPS = 1e-6
SUBLN_EPS = 1e-5
LANES = 128
SUBLANES = 8

HGRN_CHUNK = 64
HGRN_FAST_MAX_DECAY = 80.0
MASK_VALUE = -1e30
NORMS_PER_LAYER = 4

VMEM_LIMIT = 56 * 1024 * 1024

NT_DIMS = (((1,), (1,)), ((), ()))
TN_DIMS = (((0,), (0,)), ((), ()))


def _rms(x, eps):
    return x * lax.rsqrt(jnp.mean(x * x, axis=-1, keepdims=True) + eps)


def _silu(x):
    return x * jax.nn.sigmoid(x)


def _params(*sem):
    return pltpu.CompilerParams(dimension_semantics=sem, vmem_limit_bytes=VMEM_LIMIT)


def _norm_kernel(x_ref, nw_ref, h_ref):
    h_ref[...] = (_rms(x_ref[...], NORM_EPS) * nw_ref[...]).astype(h_ref.dtype)


def _norm(x, nw, *, tm):
    s, d = x.shape
    return pl.pallas_call(
        _norm_kernel,
        grid=(s // tm,),
        in_specs=[pl.BlockSpec((tm, d), lambda i: (i, 0)), pl.BlockSpec((1, d), lambda i: (0, 0))],
        out_specs=pl.BlockSpec((tm, d), lambda i: (i, 0)),
        out_shape=jax.ShapeDtypeStruct((s, d), BF16),
        compiler_params=_params("parallel"),
        name="pre_norm",
    )(x, nw)


def _inproj_kernel(h_ref, w_ref, tab_ref, o_ref, wb_ref, *, rope_lo, rope_hi):
    j = pl.program_id(0)

    @pl.when(pl.program_id(1) == 0)
    def _():
        wb_ref[...] = w_ref[...].astype(BF16)

    is_rope = jnp.logical_and(j >= rope_lo, j < rope_hi)

    @pl.when(is_rope)
    def _():
        acc = jnp.dot(h_ref[...], wb_ref[...], preferred_element_type=F32)
        c = tab_ref[0, 0]
        s_dn = tab_ref[0, 1]
        s_up = tab_ref[0, 2]
        for g in range(acc.shape[1] // LANES):
            t = acc[:, g * LANES:(g + 1) * LANES]
            r = (t * c + pltpu.roll(t, LANES - ROPE_DIM // 2, 1) * s_dn
                 + pltpu.roll(t, ROPE_DIM // 2, 1) * s_up)
            o_ref[:, g * LANES:(g + 1) * LANES] = r.astype(o_ref.dtype)

    @pl.when(jnp.logical_not(is_rope))
    def _():
        o_ref[...] = jnp.dot(h_ref[...], wb_ref[...],
                             preferred_element_type=F32).astype(o_ref.dtype)


def _in_proj(h, w_all, tab, *, layer, tm, tn, q_col, k_col, v_col):
    s, d = h.shape
    n = w_all.shape[2]
    assert q_col % tn == 0 and k_col - q_col == tn and v_col - k_col == tn
    q_blk = q_col // tn

    def tab_map(j, i):
        is_rope = jnp.logical_and(j >= q_blk, j < q_blk + 2)
        return (jnp.clip(j - q_blk, 0, 1), 0, jnp.where(is_rope, i, 0), 0)

    return pl.pallas_call(
        functools.partial(_inproj_kernel, rope_lo=q_blk, rope_hi=q_blk + 2),
        grid=(n // tn, s // tm),
        in_specs=[
            pl.BlockSpec((tm, d), lambda j, i: (i, 0)),
            pl.BlockSpec((None, d, tn), lambda j, i: (layer, 0, j)),
            pl.BlockSpec((1, 3, tm, LANES), tab_map),
        ],
        out_specs=pl.BlockSpec((tm, tn), lambda j, i: (i, j)),
        out_shape=jax.ShapeDtypeStruct((s, n), BF16),
        scratch_shapes=[pltpu.VMEM((d, tn), BF16)],
        compiler_params=_params("arbitrary", "arbitrary"),
        name="in_proj",
    )(h, w_all, tab)


def _chunk_cumsum(x):
    groups = x.shape[0] // SUBLANES
    x3 = x.reshape(groups, SUBLANES, x.shape[1])
    sub = lax.broadcasted_iota(jnp.int32, (1, SUBLANES, x.shape[1]), 1)
    for shift in (1, 2, 4):
        x3 = x3 + jnp.where(sub >= shift, pltpu.roll(x3, shift, 1), 0.0)
    tot = x3[:, SUBLANES - 1:SUBLANES, :]
    offs = [jnp.zeros_like(tot[0:1])]
    for g in range(1, groups):
        offs.append(offs[-1] + tot[g - 1:g])
    return (x3 + jnp.concatenate(offs, axis=0)).reshape(x.shape)


def _hgrn_kernel(lbraw_ref, gw_ref, q_ref, f_ref, i_ref, g_ref, *rest, layer, tt, n_cast):
    cast_in, (o_ref, *cast_out) = rest[:n_cast], rest[n_cast:2 * n_cast + 1]
    (state_ref, lbp_ref, qg_ref, kinv_ref, kdec_ref, eg_ref,
     osc_ref, gsc_ref, ksc_ref, vsc_ref) = rest[2 * n_cast + 1:]
    c_len = HGRN_CHUNK
    n_chunks = tt // c_len
    dh = HGRN_DIM
    width = HGRN_WIDTH

    for src_ref, dst_ref in zip(cast_in, cast_out):
        dst_ref[...] = src_ref[...].astype(dst_ref.dtype)

    @pl.when(pl.program_id(0) == 0)
    def _():
        state_ref[...] = jnp.zeros_like(state_ref)
        gsc_ref[...] = jnp.zeros_like(gsc_ref)
        ksc_ref[...] = jnp.zeros_like(ksc_ref)
        vsc_ref[...] = jnp.zeros_like(vsc_ref)
        raw = lbraw_ref[...]
        depth = raw.shape[0]
        mx = raw[0:1]
        for r in range(1, depth):
            mx = jnp.maximum(mx, raw[r:r + 1])
        es = [jnp.exp(raw[r:r + 1] - mx) for r in range(depth)]
        tot = es[0]
        for r in range(1, depth):
            tot = tot + es[r]
        lb = jnp.zeros_like(tot)
        for r in range(1, layer + 1):
            lb = lb + es[r] / tot
        lbp_ref[...] = jnp.maximum(lb, jnp.finfo(F32).tiny)

    causal = (lax.broadcasted_iota(jnp.int32, (c_len, c_len), 0)
              >= lax.broadcasted_iota(jnp.int32, (c_len, c_len), 1))
    lbp = lbp_ref[...]
    gw = gw_ref[...]

    def chunk_rows(c):
        return pl.ds(pl.multiple_of(c * c_len, c_len), c_len)

    def gates(rows):
        qr = q_ref[rows, :].astype(F32)
        fr = f_ref[rows, :].astype(F32)
        t = jnp.exp(-jnp.abs(fr))
        pos = fr >= 0.0
        inv = 1.0 / (1.0 + t)
        f = jnp.where(pos, 1.0 + lbp * t, t + lbp) * inv
        kk = jnp.where(pos, t, 1.0) * inv * (1.0 - lbp)
        return _silu(qr), kk, _chunk_cumsum(jnp.log2(f))

    def prepare(c, gmin):
        rows = chunk_rows(c)
        q, kk, gcum = gates(rows)
        glast = gcum[c_len - 1:c_len, :]
        qg_ref[rows, :] = (q * jnp.exp2(gcum)).astype(BF16)
        kinv_ref[rows, :] = (kk * jnp.exp2(-gcum)).astype(BF16)
        kdec_ref[rows, :] = (kk * jnp.exp2(glast - gcum)).astype(BF16)
        eg_ref[c] = jnp.exp2(glast)
        return jnp.minimum(gmin, glast)

    gmin = lax.fori_loop(0, n_chunks, prepare, jnp.zeros((1, width), F32))
    fast = jnp.min(gmin) > -HGRN_FAST_MAX_DECAY * math.log2(math.e)

    heads = [slice(h * dh, (h + 1) * dh) for h in range(HGRN_HEADS)]

    def finish_heads(c, rows, o_intra):
        states = [state_ref[h] for h in range(HGRN_HEADS)]
        o_inter = [lax.dot_general(qg_ref[rows, sl], st.astype(BF16), NT_DIMS,
                                   preferred_element_type=F32) for sl, st in zip(heads, states)]
        upd = [lax.dot_general(i_ref[rows, sl], kdec_ref[rows, sl], TN_DIMS,
                               preferred_element_type=F32) for sl in heads]
        for h, sl in enumerate(heads):
            state_ref[h] = states[h] * eg_ref[c, :, sl] + upd[h]
        for h, sl in enumerate(heads):
            gate = _silu(g_ref[rows, sl].astype(F32))
            o = o_intra[h] + o_inter[h]
            o_ref[rows, sl] = (_rms(o, NORM_EPS) * gw * gate).astype(o_ref.dtype)

    def fast_chunk(c, carry):
        rows = chunk_rows(c)
        a = [lax.dot_general(qg_ref[rows, sl], kinv_ref[rows, sl], NT_DIMS,
                             preferred_element_type=F32) for sl in heads]
        a = [jnp.where(causal, ah, 0.0).astype(BF16) for ah in a]
        o_intra = [jnp.dot(ah, i_ref[rows, sl], preferred_element_type=F32)
                   for ah, sl in zip(a, heads)]
        finish_heads(c, rows, o_intra)
        return carry

    def slow_chunk(c, carry):
        rows = chunk_rows(c)
        q, kk, gcum = gates(rows)
        gsc_ref[c_len:, :] = gcum
        ksc_ref[c_len:, :] = kk
        vsc_ref[c_len:, :] = i_ref[rows, :].astype(F32)
        osc_ref[...] = jnp.zeros_like(osc_ref)

        def off_body(d, carry2):
            gs = pltpu.roll(gsc_ref[...], d, 0)[c_len:, :]
            ks = pltpu.roll(ksc_ref[...], d, 0)[c_len:, :]
            vs = pltpu.roll(vsc_ref[...], d, 0)[c_len:, :]
            p = q * ks * jnp.exp2(jnp.minimum(gcum - gs, 0.0))
            for h in range(HGRN_HEADS):
                sl = slice(h * dh, (h + 1) * dh)
                osc_ref[:, sl] += jnp.sum(p[:, sl], axis=1, keepdims=True) * vs[:, sl]
            return carry2

        lax.fori_loop(0, c_len, off_body, 0)
        finish_heads(c, rows, [osc_ref[:, sl] for sl in heads])
        return carry

    @pl.when(fast)
    def _():
        lax.fori_loop(0, n_chunks, fast_chunk, 0, unroll=4)

    @pl.when(jnp.logical_not(fast))
    def _():
        lax.fori_loop(0, n_chunks, slow_chunk, 0)


def _hgrn(proj, lb_raw, gnorm_w, cast_weights, *, layer, tt):
    s = proj.shape[0]
    w = HGRN_WIDTH
    depth = lb_raw.shape[0]
    steps = s // tt
    col_spec = lambda b: pl.BlockSpec((tt, w), lambda i, b=b: (i, b))
    cast_in, cast_out, cast_shapes = [], [], []
    for cw in cast_weights:
        rows, cols = cw.shape[1:]
        slab = rows // steps
        assert slab * steps == rows and slab % (2 * SUBLANES) == 0
        cast_in.append(pl.BlockSpec((None, slab, cols), lambda i: (layer, i, 0)))
        cast_out.append(pl.BlockSpec((slab, cols), lambda i: (i, 0)))
        cast_shapes.append(jax.ShapeDtypeStruct((rows, cols), BF16))
    return pl.pallas_call(
        functools.partial(_hgrn_kernel, layer=layer, tt=tt, n_cast=len(cast_weights)),
        grid=(steps,),
        in_specs=[
            pl.BlockSpec((depth, w), lambda i: (0, 0)),
            pl.BlockSpec((1, HGRN_DIM), lambda i: (0, 0)),
            col_spec(0), col_spec(1), col_spec(2), col_spec(3),
        ] + cast_in,
        out_specs=[pl.BlockSpec((tt, w), lambda i: (i, 0))] + cast_out,
        out_shape=[jax.ShapeDtypeStruct((s, w), BF16)] + cast_shapes,
        scratch_shapes=[
            pltpu.VMEM((HGRN_HEADS, HGRN_DIM, HGRN_DIM), F32),
            pltpu.VMEM((1, w), F32),
            pltpu.VMEM((tt, w), BF16),
            pltpu.VMEM((tt, w), BF16),
            pltpu.VMEM((tt, w), BF16),
            pltpu.VMEM((tt // HGRN_CHUNK, 1, w), F32),
            pltpu.VMEM((HGRN_CHUNK, w), F32),
            pltpu.VMEM((2 * HGRN_CHUNK, w), F32),
            pltpu.VMEM((2 * HGRN_CHUNK, w), F32),
            pltpu.VMEM((2 * HGRN_CHUNK, w), F32),
        ],
        compiler_params=_params("arbitrary"),
        name="hgrn",
    )(lb_raw, gnorm_w, proj, proj, proj, proj, *cast_weights)


def _attn_kernel(lam_ref, sw_ref, q_ref, k_ref, v_ref, o_ref,
                 s0_ref, s1_ref, p0_ref, p1_ref, al0_ref, al1_ref, m_ref, l_ref, acc_ref,
                 *, lam_init, tq, rb):
    s_refs, p_refs, al_refs = (s0_ref, s1_ref), (p0_ref, p1_ref), (al0_ref, al1_ref)
    qi = pl.program_id(1)
    d = DIFF_HEAD_DIM
    n_col = tq // LANES
    m_ref[...] = jnp.full_like(m_ref, MASK_VALUE)
    l_ref[...] = jnp.zeros_like(l_ref)
    acc_ref[...] = jnp.zeros_like(acc_ref)

    def softmax_rows(masked, s_ref, p_ref, al_ref):
        def load_scores(mp, r0):
            s = s_ref[mp, pl.ds(r0, rb), :]
            if masked:
                rchunk = (r0 + lax.broadcasted_iota(jnp.int32, (rb, tq), 0)) // CHUNK
                cchunk = lax.broadcasted_iota(jnp.int32, (rb, tq), 1) // CHUNK
                s = jnp.where(rchunk >= cchunk, s, MASK_VALUE)
            return s

        def max_body(r, carry):
            r0 = pl.multiple_of(r * rb, rb)
            rows = pl.ds(r0, rb)
            for mp in range(2):
                m_old = m_ref[mp, rows, :]
                m_new = jnp.maximum(m_old, jnp.max(load_scores(mp, r0), axis=1, keepdims=True))
                al_ref[mp, rows, :] = jnp.exp2(m_old - m_new)
                m_ref[mp, rows, :] = m_new
            return carry

        def exp_body(r, carry):
            r0 = pl.multiple_of(r * rb, rb)
            rows = pl.ds(r0, rb)
            for mp in range(2):
                s = load_scores(mp, r0)
                m_new = m_ref[mp, rows, :]
                psum = None
                for c in range(n_col):
                    cols = slice(c * LANES, (c + 1) * LANES)
                    pc = jnp.exp2(s[:, cols] - m_new)
                    p_ref[mp, rows, cols] = pc.astype(BF16)
                    psum = pc if psum is None else psum + pc
                l_ref[mp, rows, :] = (al_ref[mp, rows, :] * l_ref[mp, rows, :]
                                      + jnp.sum(psum, axis=1, keepdims=True))
            return carry

        lax.fori_loop(0, tq // rb, max_body, 0, unroll=True)
        lax.fori_loop(0, tq // rb, exp_body, 0, unroll=True)

    def scores(kb, s_ref):
        k = k_ref[pl.ds(pl.multiple_of(kb * tq, tq), tq), :]
        s_ref[0] = lax.dot_general(q_ref[:, :d], k[:, :d], NT_DIMS, preferred_element_type=F32)
        s_ref[1] = lax.dot_general(q_ref[:, d:], k[:, d:], NT_DIMS, preferred_element_type=F32)

    def accumulate(kb, p_ref, al_ref):
        v = v_ref[pl.ds(pl.multiple_of(kb * tq, tq), tq), :]
        for mp in range(2):
            pv = jnp.dot(p_ref[mp], v, preferred_element_type=F32)
            al = al_ref[mp]
            for c in range(DIFF_V_DIM // LANES):
                cols = slice(c * LANES, (c + 1) * LANES)
                acc_ref[mp, :, cols] = al * acc_ref[mp, :, cols] + pv[:, cols]

    def trip(kb, cur):
        oth = 1 - cur
        scores(kb + 1, s_refs[oth])
        accumulate(jnp.maximum(kb - 1, 0), p_refs[oth], al_refs[oth])
        softmax_rows(False, s_refs[cur], p_refs[cur], al_refs[cur])

    def drain(cur):
        oth = 1 - cur
        accumulate(jnp.maximum(qi - 1, 0), p_refs[oth], al_refs[oth])
        softmax_rows(True, s_refs[cur], p_refs[cur], al_refs[cur])
        accumulate(qi, p_refs[cur], al_refs[cur])

    p_refs[1][...] = jnp.zeros_like(p_refs[1])
    al_refs[1][...] = jnp.zeros_like(al_refs[1])
    scores(0, s_refs[0])

    def trip_body(kb, carry):
        even = kb % 2 == 0

        @pl.when(even)
        def _():
            trip(kb, 0)

        @pl.when(jnp.logical_not(even))
        def _():
            trip(kb, 1)

        return carry

    lax.fori_loop(0, qi, trip_body, 0)
    odd = qi % 2 == 1

    @pl.when(odd)
    def _():
        drain(1)

    @pl.when(jnp.logical_not(odd))
    def _():
        drain(0)

    lp = lam_ref[...]
    lam = (jnp.exp(jnp.sum(lp[0:1] * lp[1:2], axis=1, keepdims=True))
           - jnp.exp(jnp.sum(lp[2:3] * lp[3:4], axis=1, keepdims=True)) + lam_init)
    w0 = 1.0 / l_ref[0]
    w1 = lam * (1.0 / l_ref[1])
    halves = []
    ssq = None
    for c in range(DIFF_V_DIM // LANES):
        cols = slice(c * LANES, (c + 1) * LANES)
        oc = acc_ref[0, :, cols] * w0 - acc_ref[1, :, cols] * w1
        halves.append(oc)
        sq = jnp.sum(oc * oc, axis=1, keepdims=True)
        ssq = sq if ssq is None else ssq + sq
    scale = lax.rsqrt(ssq * (1.0 / DIFF_V_DIM) + SUBLN_EPS) * (1.0 - lam_init)
    for c, oc in enumerate(halves):
        cols = slice(c * LANES, (c + 1) * LANES)
        o_ref[:, cols] = (oc * scale * sw_ref[:, cols]).astype(o_ref.dtype)


def _attn(proj, lam_all, subln_all, *, layer, lam_init, tq, rb, q_col, k_col, v_col):
    s = proj.shape[0]
    hw = 2 * DIFF_HEAD_DIM
    qb, kb, vb = q_col // hw, k_col // hw, v_col // hw
    return pl.pallas_call(
        functools.partial(_attn_kernel, lam_init=lam_init, tq=tq, rb=rb),
        grid=(DIFF_HEADS, s // tq),
        in_specs=[
            pl.BlockSpec((None,) + lam_all.shape[1:], lambda h, i: (layer, 0, 0)),
            pl.BlockSpec((None, 1, DIFF_V_DIM), lambda h, i: (layer, 0, 0)),
            pl.BlockSpec((tq, hw), lambda h, i: (i, qb + h)),
            pl.BlockSpec((s, hw), lambda h, i: (0, kb + h)),
            pl.BlockSpec((s, DIFF_V_DIM), lambda h, i: (0, vb + h)),
        ],
        out_specs=pl.BlockSpec((tq, DIFF_V_DIM), lambda h, i: (i, h)),
        out_shape=jax.ShapeDtypeStruct((s, DIFF_HEADS * DIFF_V_DIM), BF16),
        scratch_shapes=[
            pltpu.VMEM((2, tq, tq), F32),
            pltpu.VMEM((2, tq, tq), F32),
            pltpu.VMEM((2, tq, tq), BF16),
            pltpu.VMEM((2, tq, tq), BF16),
            pltpu.VMEM((2, tq, LANES), F32),
            pltpu.VMEM((2, tq, LANES), F32),
            pltpu.VMEM((2, tq, LANES), F32),
            pltpu.VMEM((2, tq, LANES), F32),
            pltpu.VMEM((2, tq, DIFF_V_DIM), F32),
        ],
        compiler_params=_params("parallel", "arbitrary"),
        name="diff_attn",
    )(lam_all, subln_all, proj, proj, proj)


def _merge_kernel(x_ref, oa_ref, ob_ref, ga0_ref, ga1_ref, gb0_ref, gb1_ref,
                  wa_ref, wb_ref, wo_ref, nwo_ref, nwn_ref, out_ref, h_ref, mg_ref):
    half = ga0_ref.shape[1]
    oa = oa_ref[...]
    ob = ob_ref[...]
    for part, (ga_ref, gb_ref) in enumerate(((ga0_ref, gb0_ref), (ga1_ref, gb1_ref))):
        sl = slice(part * half, (part + 1) * half)
        ya = jnp.dot(oa, wa_ref[:, sl], preferred_element_type=F32)
        yb = jnp.dot(ob, wb_ref[:, sl], preferred_element_type=F32)
        merged = (jax.nn.sigmoid(ga_ref[...].astype(F32)) * ya
                  + jax.nn.sigmoid(gb_ref[...].astype(F32)) * yb)
        mg_ref[:, sl] = merged.astype(BF16)
    z = jnp.dot(mg_ref[...], wo_ref[...], preferred_element_type=F32)
    x = x_ref[...] + _rms(z, NORM_EPS) * nwo_ref[...]
    out_ref[...] = x
    h_ref[...] = (_rms(x, NORM_EPS) * nwn_ref[...]).astype(h_ref.dtype)


def _merge(x, o_a, o_b, proj, wa, wb, wo, nw_all, *, layer, tm, gate_col):
    s, d = x.shape
    half = d // 2
    gb = gate_col // half
    wa_k, wb_k = wa.shape[0], wb.shape[0]
    const = lambda rows: pl.BlockSpec((rows, d), lambda i: (0, 0), pipeline_mode=pl.Buffered(1))
    nw_row = lambda r: pl.BlockSpec((None, 1, d), lambda i, r=r: (NORMS_PER_LAYER * layer + r, 0, 0))
    gate = lambda b: pl.BlockSpec((tm, half), lambda i, b=b: (i, b))
    return pl.pallas_call(
        _merge_kernel,
        grid=(s // tm,),
        in_specs=[
            pl.BlockSpec((tm, d), lambda i: (i, 0)),
            pl.BlockSpec((tm, wa_k), lambda i: (i, 0)),
            pl.BlockSpec((tm, wb_k), lambda i: (i, 0)),
            gate(gb), gate(gb + 1), gate(gb + 2), gate(gb + 3),
            const(wa_k), const(wb_k), const(d), nw_row(1), nw_row(2),
        ],
        out_specs=[pl.BlockSpec((tm, d), lambda i: (i, 0)), pl.BlockSpec((tm, d), lambda i: (i, 0))],
        out_shape=[jax.ShapeDtypeStruct((s, d), F32), jax.ShapeDtypeStruct((s, d), BF16)],
        scratch_shapes=[pltpu.VMEM((tm, d), BF16)],
        compiler_params=_params("parallel"),
        name="merge",
    )(x, o_a, o_b, proj, proj, proj, proj, wa, wb, wo, nw_all, nw_all)


def _ffn_up_kernel(h_ref, wg_ref, wu_ref, wd_ref, a_ref, wd16_ref, wgb_ref, wub_ref):
    @pl.when(pl.program_id(1) == 0)
    def _():
        wgb_ref[...] = wg_ref[...].astype(BF16)
        wub_ref[...] = wu_ref[...].astype(BF16)
        wd16_ref[...] = wd_ref[...].astype(wd16_ref.dtype)

    h = h_ref[...]
    g = jnp.dot(h, wgb_ref[...], preferred_element_type=F32)
    u = jnp.dot(h, wub_ref[...], preferred_element_type=F32)
    a_ref[...] = (_silu(g) * u).astype(a_ref.dtype)


def _ffn_up(h, w_all, wd_all, *, layer, tm, th):
    s, d = h.shape
    hidden = w_all.shape[2] // 2
    nh = hidden // th
    return pl.pallas_call(
        _ffn_up_kernel,
        grid=(nh, s // tm),
        in_specs=[
            pl.BlockSpec((tm, d), lambda j, i: (i, 0)),
            pl.BlockSpec((None, d, th), lambda j, i: (layer, 0, j)),
            pl.BlockSpec((None, d, th), lambda j, i: (layer, 0, nh + j)),
            pl.BlockSpec((None, th, d), lambda j, i: (layer, j, 0)),
        ],
        out_specs=[pl.BlockSpec((tm, th), lambda j, i: (i, j)),
                   pl.BlockSpec((th, d), lambda j, i: (j, 0))],
        out_shape=[jax.ShapeDtypeStruct((s, hidden), BF16),
                   jax.ShapeDtypeStruct((hidden, d), BF16)],
        scratch_shapes=[pltpu.VMEM((d, th), BF16), pltpu.VMEM((d, th), BF16)],
        compiler_params=_params("arbitrary", "arbitrary"),
        name="ffn_up",
    )(h, w_all, w_all, wd_all)


def _ffn_down_kernel(x_ref, a_ref, wd_ref, nwo_ref, *rest, emit_next):
    z = jnp.dot(a_ref[...], wd_ref[...], preferred_element_type=F32)
    x = x_ref[...] + _rms(z, NORM_EPS) * nwo_ref[...]
    if emit_next:
        nwn_ref, out_ref, h_ref = rest
        h_ref[...] = (_rms(x, NORM_EPS) * nwn_ref[...]).astype(h_ref.dtype)
    else:
        (out_ref,) = rest
    out_ref[...] = x


def _ffn_down(x, a, wd, nw_all, *, layer, tm):
    s, d = x.shape
    hidden = a.shape[1]
    emit_next = NORMS_PER_LAYER * (layer + 1) < nw_all.shape[0]
    row = pl.BlockSpec((tm, d), lambda i: (i, 0))
    in_specs = [
        row,
        pl.BlockSpec((tm, hidden), lambda i: (i, 0)),
        pl.BlockSpec((hidden, d), lambda i: (0, 0), pipeline_mode=pl.Buffered(1)),
        pl.BlockSpec((None, 1, d), lambda i: (NORMS_PER_LAYER * layer + 3, 0, 0)),
    ]
    args = [x, a, wd, nw_all]
    out_specs, out_shape = [row], [jax.ShapeDtypeStruct((s, d), F32)]
    if emit_next:
        in_specs.append(pl.BlockSpec((None, 1, d), lambda i: (NORMS_PER_LAYER * (layer + 1), 0, 0)))
        args.append(nw_all)
        out_specs.append(row)
        out_shape.append(jax.ShapeDtypeStruct((s, d), BF16))
    outs = pl.pallas_call(
        functools.partial(_ffn_down_kernel, emit_next=emit_next),
        grid=(s // tm,),
        in_specs=in_specs,
        out_specs=out_specs,
        out_shape=out_shape,
        compiler_params=_params("parallel"),
        name="ffn_down",
    )(*args)
    return (outs[0], outs[1]) if emit_next else (outs[0], None)


def _rope_tables(seq):
    half = ROPE_DIM // 2
    pos = jnp.arange(seq, dtype=F32)
    inv_freq = ROPE_THETA ** (-jnp.arange(0, ROPE_DIM, 2, dtype=F32) / ROPE_DIM)
    ang = pos[:, None] * inv_freq[None, :]
    cos, sin = jnp.cos(ang), jnp.sin(ang)
    zeros = jnp.zeros((seq, LANES - ROPE_DIM), F32)
    zh = jnp.zeros((seq, half), F32)
    c = jnp.concatenate([cos, cos, jnp.ones_like(zeros)], axis=1)
    s_dn = jnp.concatenate([-sin, zh, zeros], axis=1)
    s_up = jnp.concatenate([zh, sin, zeros], axis=1)
    k_tab = jnp.stack([c, s_dn, s_up])
    return jnp.stack([k_tab * (DIFF_HEAD_DIM ** -0.5 * math.log2(math.e)), k_tab])


def kernel(x, w_in, hgrn_lower_bounds, hgrn_gnorm_w, diff_lambda, diff_subln_w,
           w_branch_a, w_branch_b, w_out, norm_w, w_ffn_in, w_ffn_out):
    batch, seq, d_model = x.shape
    assert batch == 1
    depth = w_in.shape[0]
    q_col = 4 * HGRN_WIDTH
    k_col = q_col + DIFF_HEADS * 2 * DIFF_HEAD_DIM
    v_col = k_col + DIFF_HEADS * 2 * DIFF_HEAD_DIM
    gate_col = v_col + DIFF_HEADS * DIFF_V_DIM
    tile = {name: min(t, seq) for name, t in dict(
        norm=512, in_proj=1024, hgrn=512, attn=512, merge=512, ffn_up=1024, ffn_down=256).items()}
    tab = _rope_tables(seq)
    xs = x[0]
    assert norm_w.shape[1] == NORMS_PER_LAYER
    nw_all = norm_w.astype(F32).reshape(depth * NORMS_PER_LAYER, 1, d_model)
    lb_raw = hgrn_lower_bounds.astype(F32)
    gnorm = hgrn_gnorm_w.astype(F32)
    lam_all = diff_lambda.astype(F32)
    subln_all = diff_subln_w.astype(F32)[:, None, :]
    h = _norm(xs, nw_all[0], tm=tile["norm"])
    for l in range(depth):
        lam_init = 0.8 - 0.6 * math.exp(-0.3 * l)
        proj = _in_proj(h, w_in, tab, layer=l, tm=tile["in_proj"], tn=1024,
                        q_col=q_col, k_col=k_col, v_col=v_col)
        o_a, wa16, wb16, wo16 = _hgrn(proj, lb_raw, gnorm[l:l + 1], (w_branch_a, w_branch_b, w_out),
                                      layer=l, tt=tile["hgrn"])
        o_b = _attn(proj, lam_all, subln_all, layer=l, lam_init=lam_init, tq=tile["attn"], rb=16,
                    q_col=q_col, k_col=k_col, v_col=v_col)
        xs, hf = _merge(xs, o_a, o_b, proj, wa16, wb16, wo16, nw_all,
                        layer=l, tm=tile["merge"], gate_col=gate_col)
        a, wd16 = _ffn_up(hf, w_ffn_in, w_ffn_out, layer=l, tm=tile["ffn_up"], th=512)
        xs, h = _ffn_down(xs, a, wd16, nw_all, layer=l, tm=tile["ffn_down"])
    return xs[None]
```
